```python
import jax, jax.numpy as jnp
from jax import lax
import numpy as np

D_MODEL = 2048
BATCH = 16
SEQ = 2048
DEPTH = 2

CHUNK = 128
GM_WIDTH = D_MODEL // 4
GM_HEAD_DIM = 128
GM_HEADS = GM_WIDTH // GM_HEAD_DIM
ATT_WIDTH = D_MODEL // 4
ATT_HEAD_DIM = 64
ATT_HEADS = ATT_WIDTH // ATT_HEAD_DIM
ATT_KV_HEADS = ATT_HEADS // 4
KV_WIDTH = ATT_KV_HEADS * ATT_HEAD_DIM
WINDOW = 128
SSM_WIDTH = D_MODEL // 2
SSM_HEAD_DIM = 64
SSM_HEADS = SSM_WIDTH // SSM_HEAD_DIM
SSM_GROUPS = 2
SSM_STATE = 128
CONV_WIDTH = 4
BC_WIDTH = SSM_GROUPS * SSM_STATE
CONV_CH = SSM_WIDTH + 2 * BC_WIDTH
MIX_WIDTH = GM_WIDTH + ATT_WIDTH + SSM_WIDTH
IN_SIZES = (GM_WIDTH, GM_WIDTH, ATT_WIDTH, KV_WIDTH, KV_WIDTH, SSM_WIDTH, CONV_CH, SSM_HEADS)
IN_WIDTH = 2 * GM_WIDTH + ATT_WIDTH + 2 * KV_WIDTH + SSM_WIDTH + CONV_CH + SSM_HEADS
D_FF = 4 * D_MODEL
NEG_INF = -1e30
EPS = 1e-6

kernel_name = 'hybrid_gmlp_swa_ssd_parallel_heads'


def rms_norm(x, g):
    xf = x.astype(jnp.float32)
    y = xf * lax.rsqrt(jnp.mean(xf * xf, axis=-1, keepdims=True) + EPS)
    return (y * g.astype(jnp.float32)).astype(x.dtype)


def layer_norm(x, g, b):
    xf = x.astype(jnp.float32)
    mu = jnp.mean(xf, axis=-1, keepdims=True)
    xc = xf - mu
    y = xc * lax.rsqrt(jnp.mean(xc * xc, axis=-1, keepdims=True) + 1e-5)
    return (y * g.astype(jnp.float32) + b.astype(jnp.float32)).astype(x.dtype)


def split_projection(p):
    offs = np.cumsum(np.array(IN_SIZES))[:-1].tolist()
    return jnp.split(p, offs, axis=-1)


def spatial_gating_mixer(u, v, ln_g, ln_b, w_s, b_s, out_g):
    bsz, s, _ = u.shape
    nc = s // CHUNK
    u = jax.nn.gelu(u).reshape(bsz, nc, CHUNK, GM_HEADS, GM_HEAD_DIM)
    v = jax.nn.gelu(v).reshape(bsz, s, GM_HEADS, GM_HEAD_DIM)
    v = layer_norm(v, ln_g, ln_b).reshape(bsz, nc, CHUNK, GM_HEADS, GM_HEAD_DIM)
    w = jnp.tril(w_s).astype(v.dtype)
    gate = jnp.einsum('hts,bcshe->bcthe', w, v) + b_s.T.astype(v.dtype)[None, None, :, :, None]
    y = (u * gate).reshape(bsz, s, GM_WIDTH)
    return rms_norm(y, out_g)


def sliding_window_sink_attention(q, k, v, sinks, out_g):
    bsz, s, _ = q.shape
    nb = s // WINDOW
    grp = ATT_HEADS // ATT_KV_HEADS
    qb = q.reshape(bsz, nb, WINDOW, ATT_KV_HEADS, grp, ATT_HEAD_DIM)
    pad = ((0, 0), (WINDOW, 0), (0, 0))
    kp = jnp.pad(k, pad).reshape(bsz, nb + 1, WINDOW, ATT_KV_HEADS, ATT_HEAD_DIM)
    vp = jnp.pad(v, pad).reshape(bsz, nb + 1, WINDOW, ATT_KV_HEADS, ATT_HEAD_DIM)
    kb = jnp.concatenate([kp[:, :-1], kp[:, 1:]], axis=2)
    vb = jnp.concatenate([vp[:, :-1], vp[:, 1:]], axis=2)
    scores = jnp.einsum('bnqkgd,bnjkd->bnkgqj', qb, kb).astype(jnp.float32) * (ATT_HEAD_DIM ** -0.5)
    qi = jnp.arange(WINDOW)[:, None]
    kj = jnp.arange(2 * WINDOW)[None, :]
    diff = qi + WINDOW - kj
    band = (diff >= 0) & (diff < WINDOW)
    blk = jnp.arange(nb)[:, None, None]
    valid = band[None] & ((blk * WINDOW + kj[None] - WINDOW) >= 0)
    scores = jnp.where(valid[None, :, None, None], scores, NEG_INF)
    sink = sinks.astype(jnp.float32).reshape(ATT_KV_HEADS, grp)[None, None, :, :, None, None]
    m = jnp.maximum(jnp.max(scores, axis=-1, keepdims=True), sink)
    e = jnp.exp(scores - m)
    p = e / (jnp.sum(e, axis=-1, keepdims=True) + jnp.exp(sink - m))
    o = jnp.einsum('bnkgqj,bnjkd->bnqkgd', p.astype(v.dtype), vb)
    return rms_norm(o.reshape(bsz, s, ATT_WIDTH), out_g)


def causal_depthwise_conv(x, w, b):
    kern = w[:, None, :].astype(x.dtype)
    out = lax.conv_general_dilated(x, kern, window_strides=(1,), padding=((CONV_WIDTH - 1, 0),),
                                   dimension_numbers=('NWC', 'WIO', 'NWC'),
                                   feature_group_count=x.shape[-1])
    return out + b.astype(x.dtype)


def ssd_chunked_scan(xs, dt, a_log, bm, cm, d_skip):
    bsz, s, nh, hp = xs.shape
    nc = s // CHUNK
    hg = nh // SSM_GROUPS
    a = -jnp.exp(a_log.astype(jnp.float32))
    xf = xs.astype(jnp.float32)
    xdt = (xf * dt[..., None]).reshape(bsz, nc, CHUNK, SSM_GROUPS, hg, hp)
    da = (dt * a).reshape(bsz, nc, CHUNK, SSM_GROUPS, hg).transpose(0, 1, 3, 4, 2)
    bc = bm.astype(jnp.float32).reshape(bsz, nc, CHUNK, SSM_GROUPS, SSM_STATE)
    cc = cm.astype(jnp.float32).reshape(bsz, nc, CHUNK, SSM_GROUPS, SSM_STATE)
    a_cs = jnp.cumsum(da, axis=-1)
    idx = jnp.arange(CHUNK)
    causal = idx[:, None] >= idx[None, :]
    decay = jnp.exp(jnp.where(causal, a_cs[..., :, None] - a_cs[..., None, :], -jnp.inf))
    cb = jnp.einsum('bclgn,bcsgn->bcgls', cc, bc)
    y_diag = jnp.einsum('bcghls,bcsghp->bclghp', cb[:, :, :, None] * decay, xdt)
    decay_states = jnp.exp(a_cs[..., -1:] - a_cs)
    states = jnp.einsum('bclgn,bcghl,bclghp->bcghpn', bc, decay_states, xdt)
    chunk_decay = jnp.exp(a_cs[..., -1])

    def step(carry, inp):
        st, dec = inp
        return carry * dec[..., None, None] + st, carry

    init = jnp.zeros((bsz, SSM_GROUPS, hg, hp, SSM_STATE), jnp.float32)
    _, prev = lax.scan(step, init, (jnp.moveaxis(states, 1, 0), jnp.moveaxis(chunk_decay, 1, 0)))
    prev = jnp.moveaxis(prev, 0, 1)
    y_off = jnp.einsum('bclgn,bcghpn,bcghl->bclghp', cc, prev, jnp.exp(a_cs))
    y = (y_diag + y_off).reshape(bsz, s, nh, hp)
    return y + xf * d_skip.astype(jnp.float32)[:, None]


def ssd_mixer(z, xbc, dt_raw, conv_w, conv_b, dt_bias, a_log, d_skip, norm_g):
    bsz, s, _ = z.shape
    xbc = jax.nn.silu(causal_depthwise_conv(xbc, conv_w, conv_b))
    xs, bm, cm = jnp.split(xbc, [SSM_WIDTH, SSM_WIDTH + BC_WIDTH], axis=-1)
    xs = xs.reshape(bsz, s, SSM_HEADS, SSM_HEAD_DIM)
    bm = bm.reshape(bsz, s, SSM_GROUPS, SSM_STATE)
    cm = cm.reshape(bsz, s, SSM_GROUPS, SSM_STATE)
    dt = jax.nn.softplus(dt_raw.astype(jnp.float32) + dt_bias.astype(jnp.float32))
    y = ssd_chunked_scan(xs, dt, a_log, bm, cm, d_skip)
    y = y.reshape(bsz, s, SSM_WIDTH) * jax.nn.silu(z.astype(jnp.float32))
    yg = y.reshape(bsz, s, SSM_GROUPS, SSM_WIDTH // SSM_GROUPS)
    yg = yg * lax.rsqrt(jnp.mean(yg * yg, axis=-1, keepdims=True) + EPS)
    return (yg.reshape(bsz, s, SSM_WIDTH) * norm_g.astype(jnp.float32)).astype(z.dtype)


def setup_inputs(seed: int = 0) -> dict:
    key = jax.random.key(seed)
    ks = jax.random.split(key, 26)
    L, D = DEPTH, D_MODEL

    def nrm(k, shape, scale):
        return jax.random.normal(k, shape, jnp.float32) * scale

    dt0 = jnp.exp(jax.random.uniform(ks[15], (L, SSM_HEADS), jnp.float32, np.log(1e-3), np.log(1e-1)))
    return {
        'x': nrm(ks[0], (BATCH, SEQ, D), 1.0),
        'c': nrm(ks[1], (BATCH, D), 1.0),
        'ada_w': nrm(ks[2], (L, D, 6 * D), 0.5 * D ** -0.5),
        'ada_b': nrm(ks[3], (L, 6 * D), 0.02),
        'norm1_g': 1.0 + nrm(ks[4], (L, D), 0.05),
        'w_in': nrm(ks[5], (L, D, IN_WIDTH), D ** -0.5),
        'gm_ln_g': 1.0 + nrm(ks[6], (L, GM_HEADS, GM_HEAD_DIM), 0.05),
        'gm_ln_b': nrm(ks[7], (L, GM_HEADS, GM_HEAD_DIM), 0.02),
        'gm_ws': nrm(ks[8], (L, GM_HEADS, CHUNK, CHUNK), CHUNK ** -0.5),
        'gm_bs': 1.0 + nrm(ks[9], (L, GM_HEADS, CHUNK), 0.05),
        'gm_norm_g': 1.0 + nrm(ks[10], (L, GM_WIDTH), 0.05),
        'attn_sinks': nrm(ks[11], (L, ATT_HEADS), 0.5),
        'attn_norm_g': 1.0 + nrm(ks[12], (L, ATT_WIDTH), 0.05),
        'conv_w': nrm(ks[13], (L, CONV_WIDTH, CONV_CH), CONV_WIDTH ** -0.5),
        'conv_b': nrm(ks[14], (L, CONV_CH), 0.02),
        'dt_bias': dt0 + jnp.log(-jnp.expm1(-dt0)),
        'a_log': jnp.log(jax.random.uniform(ks[16], (L, SSM_HEADS), jnp.float32, 1.0, 16.0)),
        'd_skip': 1.0 + nrm(ks[17], (L, SSM_HEADS), 0.1),
        'ssm_norm_g': 1.0 + nrm(ks[18], (L, SSM_WIDTH), 0.05),
        'w_out': nrm(ks[19], (L, MIX_WIDTH, D), MIX_WIDTH ** -0.5),
        'norm2_g': 1.0 + nrm(ks[20], (L, D), 0.05),
        'w_mlp1': nrm(ks[21], (L, D, D_FF), D ** -0.5),
        'w_mlp2': nrm(ks[22], (L, D_FF, D), D_FF ** -0.5),
        'final_norm_g': 1.0 + nrm(ks[23], (D,), 0.05),
    }


def reference(x, c, ada_w, ada_b, norm1_g, w_in, gm_ln_g, gm_ln_b, gm_ws, gm_bs, gm_norm_g,
              attn_sinks, attn_norm_g, conv_w, conv_b, dt_bias, a_log, d_skip, ssm_norm_g,
              w_out, norm2_g, w_mlp1, w_mlp2, final_norm_g):
    c_act = jax.nn.silu(c)
    for l in range(DEPTH):
        mod = c_act @ ada_w[l] + ada_b[l]
        sh1, sc1, g1, sh2, sc2, g2 = [m[:, None, :] for m in jnp.split(mod, 6, axis=-1)]
        h = rms_norm(x, norm1_g[l]) * (1.0 + sc1) + sh1
        u_a, v_a, q_b, k_b, v_b, z_c, xbc_c, dt_c = split_projection(h @ w_in[l])
        out_a = spatial_gating_mixer(u_a, v_a, gm_ln_g[l], gm_ln_b[l], gm_ws[l], gm_bs[l], gm_norm_g[l])
        out_b = sliding_window_sink_attention(q_b, k_b, v_b, attn_sinks[l], attn_norm_g[l])
        out_c = ssd_mixer(z_c, xbc_c, dt_c, conv_w[l], conv_b[l], dt_bias[l], a_log[l], d_skip[l], ssm_norm_g[l])
        mix = jnp.concatenate([out_a, out_b, out_c], axis=-1) @ w_out[l]
        x = x + g1 * mix
        h = rms_norm(x, norm2_g[l]) * (1.0 + sc2) + sh2
        x = x + g2 * (jnp.square(jax.nn.relu(h @ w_mlp1[l])) @ w_mlp2[l])
    return rms_norm(x, final_norm_g)
```

```python
import functools

import numpy as np
import jax
import jax.numpy as jnp
from jax import lax
from jax.experimental import pallas as pl
from jax.experimental.pallas import tpu as pltpu

F32 = jnp.float32
BF16 = jnp.bfloat16

D_MODEL = 2048
CHUNK = 128
GM_WIDTH = 512
GM_HEADS = 4
ATT_WIDTH = 512
ATT_HEADS = 8
ATT_HEAD_DIM = 64
KV_WIDTH = 128
SSM_WIDTH = 1024
SSM_HEADS = 16
SSM_HEAD_DIM = 64
SSM_GROUPS = 2
SSM_STATE = 128
CONV_WIDTH = 4
CONV_CH = 1536
D_FF = 4 * D_MODEL
NEG_INF = -1e30
EPS = 1e-6
LN_EPS = 1e-5
LANES = 128

OFF_UV, OFF_Q, OFF_KV, OFF_Z, OFF_XBC, OFF_DT = 0, 1024, 1536, 1792, 2816, 4352
IN_PAD = OFF_DT + LANES

VMEM_LIMIT = 56 * 1024 * 1024


def _cparams(sem):
    return pltpu.CompilerParams(dimension_semantics=sem, vmem_limit_bytes=VMEM_LIMIT)


def _silu(x):
    return x * jax.nn.sigmoid(x)


def _gelu_tanh(x):
    return 0.5 * x * (1.0 + jnp.tanh(0.7978845608028654 * (x + 0.044715 * (x * x * x))))


def _mod_kernel(c_ref, w_ref, b_ref, o_ref):
    ca = _silu(c_ref[...]).astype(BF16)
    o_ref[...] = jnp.dot(ca, w_ref[...].astype(BF16), preferred_element_type=F32) + b_ref[...]


def _modulation(c, ada_w, ada_b):
    n_layers, d, n = ada_w.shape
    bsz = c.shape[0]
    tn = 1024
    return pl.pallas_call(
        _mod_kernel,
        grid=(n_layers, n // tn),
        in_specs=[
            pl.BlockSpec((bsz, d), lambda l, j: (0, 0)),
            pl.BlockSpec((None, d, tn), lambda l, j: (l, 0, j)),
            pl.BlockSpec((None, 1, tn), lambda l, j: (l, 0, j)),
        ],
        out_specs=pl.BlockSpec((None, bsz, tn), lambda l, j: (l, 0, j)),
        out_shape=jax.ShapeDtypeStruct((n_layers, bsz, n), F32),
        compiler_params=_cparams(("arbitrary", "arbitrary")),
        name="modulation",
    )(c, ada_w, ada_b.reshape(n_layers, 1, n))


_IN_SEGS = ((OFF_UV, 1024), (OFF_Q, 512), (OFF_KV, 256), (OFF_Z, 1024), (OFF_XBC, 1536), (OFF_DT, LANES))


def _in_kernel(x_ref, g_ref, sc_ref, sh_ref, w_ref, o_uv, o_q, o_kv, o_z, o_xbc, o_dt):
    x = x_ref[...]
    h = x * lax.rsqrt(jnp.mean(x * x, axis=-1, keepdims=True) + EPS) * g_ref[...]
    h = h * (1.0 + sc_ref[...]) + sh_ref[...]
    hb = h.astype(BF16)
    for o_ref, (off, width) in zip((o_uv, o_q, o_kv, o_z, o_xbc, o_dt), _IN_SEGS):
        o_ref[...] = jnp.dot(hb, w_ref[:, off:off + width], preferred_element_type=F32).astype(o_ref.dtype)


def _mod_spec(k, rows_per_batch_tiles):
    return pl.BlockSpec((None, None, 1, D_MODEL), lambda i: (i // rows_per_batch_tiles, k, 0, 0))


def _in_proj(x2, mod_l, g, w_in_p, seq):
    t = x2.shape[0]
    tm = 512
    tiles_per_batch = seq // tm
    widths = [w for _, w in _IN_SEGS]
    dtypes = [BF16] * 5 + [F32]
    return pl.pallas_call(
        _in_kernel,
        grid=(t // tm,),
        in_specs=[
            pl.BlockSpec((tm, D_MODEL), lambda i: (i, 0)),
            pl.BlockSpec((1, D_MODEL), lambda i: (0, 0)),
            _mod_spec(1, tiles_per_batch),
            _mod_spec(0, tiles_per_batch),
            pl.BlockSpec((D_MODEL, IN_PAD), lambda i: (0, 0), pipeline_mode=pl.Buffered(1)),
        ],
        out_specs=[pl.BlockSpec((tm, w), lambda i: (i, 0)) for w in widths],
        out_shape=[jax.ShapeDtypeStruct((t, w), dt) for w, dt in zip(widths, dtypes)],
        compiler_params=_cparams(("arbitrary",)),
        name="in_proj",
    )(x2, g.reshape(1, D_MODEL), mod_l, mod_l, w_in_p)


def _split3_pack(x, lane):
    x = jnp.where(lane < SSM_HEADS, x, 0.0)
    hi = x.astype(BF16).astype(F32)
    r1 = x - hi
    mid = r1.astype(BF16).astype(F32)
    lo = r1 - mid
    packed = hi + pltpu.roll(mid, SSM_HEADS, axis=1) + pltpu.roll(lo, 2 * SSM_HEADS, axis=1)
    return packed.astype(BF16)


def _mix_kernel(sink_ref, uv_ref, q_ref, kv_ref, z_ref, xbc_ref, dt_ref,
                lng_ref, lnb_ref, ws_ref, bs_ref, gmg_ref, ang_ref,
                cw_ref, cb_ref, dtb_ref, alog_ref, dsk_ref, sng_ref, e64_ref,
                o_ref, kvprev_ref, ext_ref, state_ref, *, rows):
    s_idx = pl.program_id(1)
    nch = rows // CHUNK

    @pl.when(s_idx == 0)
    def _():
        kvprev_ref[...] = jnp.zeros_like(kvprev_ref)
        ext_ref[0:8, :] = jnp.zeros((8, CONV_CH), F32)
        state_ref[...] = jnp.zeros_like(state_ref)

    lane = lax.broadcasted_iota(jnp.int32, (CHUNK, LANES), 1)
    row = lax.broadcasted_iota(jnp.int32, (CHUNK, LANES), 0)
    causal = row >= lane
    lo_half = lane < ATT_HEAD_DIM
    tril_ones = jnp.where(causal, 1.0, 0.0).astype(BF16)

    ext_ref[8:8 + rows, :] = xbc_ref[...].astype(F32)
    acc = cb_ref[...] + cw_ref[0:1, :] * ext_ref[pl.ds(8 - 3, rows), :]
    for k in range(1, CONV_WIDTH):
        acc = acc + cw_ref[k:k + 1, :] * ext_ref[pl.ds(8 - 3 + k, rows), :]
    xconv = _silu(acc)
    ext_ref[0:8, :] = ext_ref[rows:rows + 8, :]

    a_neg = jnp.where(lane[0:1, :] < SSM_HEADS, -jnp.exp(alog_ref[...]), 0.0)
    e64 = e64_ref[...]

    qi = lax.broadcasted_iota(jnp.int32, (CHUNK, 2 * CHUNK), 0)
    kj = lax.broadcasted_iota(jnp.int32, (CHUNK, 2 * CHUNK), 1)
    diff = qi + CHUNK - kj
    band = (diff >= 0) & (diff < CHUNK)

    for c in range(nch):
        r0 = c * CHUNK
        rs = slice(r0, r0 + CHUNK)

        u = _gelu_tanh(uv_ref[rs, 0:GM_WIDTH].astype(F32))
        v = _gelu_tanh(uv_ref[rs, GM_WIDTH:2 * GM_WIDTH].astype(F32))
        ya = []
        for h in range(GM_HEADS):
            hs = slice(h * LANES, (h + 1) * LANES)
            vh = v[:, hs]
            mu = jnp.mean(vh, axis=-1, keepdims=True)
            xc = vh - mu
            var = jnp.mean(xc * xc, axis=-1, keepdims=True)
            vn = xc * lax.rsqrt(var + LN_EPS) * lng_ref[:, hs] + lnb_ref[:, hs]
            w = jnp.where(causal, ws_ref[h], 0.0).astype(BF16)
            gate = jnp.dot(w, vn.astype(BF16), preferred_element_type=F32) + bs_ref[:, hs]
            ya.append(u[:, hs] * gate)
        ya = jnp.concatenate(ya, axis=1)
        ya = ya * lax.rsqrt(jnp.mean(ya * ya, axis=-1, keepdims=True) + EPS) * gmg_ref[...]
        o_ref[rs, 0:GM_WIDTH] = ya.astype(o_ref.dtype)

        if c == 0:
            kprev = kvprev_ref[:, 0:KV_WIDTH]
            vprev = kvprev_ref[:, KV_WIDTH:2 * KV_WIDTH]
        else:
            kprev = kv_ref[r0 - CHUNK:r0, 0:KV_WIDTH]
            vprev = kv_ref[r0 - CHUNK:r0, KV_WIDTH:2 * KV_WIDTH]
        kcat = jnp.concatenate([kprev, kv_ref[rs, 0:KV_WIDTH]], axis=0)
        vcat = jnp.concatenate([vprev, kv_ref[rs, KV_WIDTH:2 * KV_WIDTH]], axis=0)
        blk = s_idx * nch + c
        valid = band & ((kj + (blk - 1) * CHUNK) >= 0)
        ob = []
        for j in range(ATT_HEADS // 2):
            qj = q_ref[rs, j * LANES:(j + 1) * LANES].astype(F32)
            q2 = jnp.concatenate([jnp.where(lo_half, qj, 0.0), jnp.where(lo_half, 0.0, qj)], axis=0).astype(BF16)
            s2 = lax.dot_general(q2, kcat, (((1,), (1,)), ((), ())), preferred_element_type=F32)
            s2 = s2 * (ATT_HEAD_DIM ** -0.5)
            halves = []
            for half in range(2):
                sink = sink_ref[j + 4 * half]
                sc = jnp.where(valid, s2[half * CHUNK:(half + 1) * CHUNK], NEG_INF)
                m = jnp.maximum(jnp.max(sc, axis=-1, keepdims=True), sink)
                e = jnp.exp(sc - m)
                den = jnp.sum(e, axis=-1, keepdims=True) + jnp.exp(sink - m)
                p = (e / den).astype(BF16)
                halves.append(jnp.dot(p, vcat, preferred_element_type=F32))
            ob.append(jnp.where(lo_half, halves[0], halves[1]))
        ob = jnp.concatenate(ob, axis=1)
        ob = ob * lax.rsqrt(jnp.mean(ob * ob, axis=-1, keepdims=True) + EPS) * ang_ref[...]
        o_ref[rs, GM_WIDTH:GM_WIDTH + ATT_WIDTH] = ob.astype(o_ref.dtype)

        xs = xconv[rs, 0:SSM_WIDTH]
        dt_in = dt_ref[rs, :] + dtb_ref[...]
        dt = jnp.maximum(dt_in, 0.0) + jnp.log1p(jnp.exp(-jnp.abs(dt_in)))
        da = dt * a_neg
        da_hi = da.astype(BF16)
        r1 = da - da_hi.astype(F32)
        da_mid = r1.astype(BF16)
        da_lo = (r1 - da_mid.astype(F32)).astype(BF16)
        a_cs = (jnp.dot(tril_ones, da_hi, preferred_element_type=F32)
                + jnp.dot(tril_ones, da_mid, preferred_element_type=F32)
                + jnp.dot(tril_ones, da_lo, preferred_element_type=F32))
        a_cs_t = a_cs.T
        ea = jnp.exp(a_cs)
        ds = jnp.exp(a_cs[CHUNK - 1:CHUNK, :] - a_cs)
        dt_e = jnp.dot(_split3_pack(dt, lane), e64, preferred_element_type=F32)
        ds_e = jnp.dot(_split3_pack(ds, lane), e64, preferred_element_type=F32)
        ea_e = jnp.dot(_split3_pack(ea, lane), e64, preferred_element_type=F32)
        xdt = xs * dt_e
        wst = (xdt * ds_e).astype(BF16)
        yc = []
        for g in range(SSM_GROUPS):
            gs = slice(g * 512, (g + 1) * 512)
            bg = xconv[rs, SSM_WIDTH + g * SSM_STATE:SSM_WIDTH + (g + 1) * SSM_STATE]
            cg = xconv[rs, SSM_WIDTH + 256 + g * SSM_STATE:SSM_WIDTH + 256 + (g + 1) * SSM_STATE]
            bgb = bg.astype(BF16)
            cgb = cg.astype(BF16)
            cb = lax.dot_general(cgb, bgb, (((1,), (1,)), ((), ())), preferred_element_type=F32)
            bgt = bg.T.astype(BF16)
            st_new = jnp.dot(bgt, wst[:, gs], preferred_element_type=F32)
            prev = state_ref[:, gs]
            y_off = jnp.dot(cgb, prev.astype(BF16), preferred_element_type=F32) * ea_e[:, gs]
            state_ref[:, gs] = prev * ea_e[CHUNK - 1:CHUNK, gs] + st_new
            for jp in range(4):
                h0 = g * 8 + 2 * jp
                ms = []
                for h in (h0, h0 + 1):
                    seg = a_cs[:, h:h + 1] - a_cs_t[h:h + 1, :]
                    ms.append(cb * jnp.exp(jnp.where(causal, seg, -jnp.inf)))
                m2 = jnp.concatenate(ms, axis=1).astype(BF16)
                xp = xdt[:, (h0 // 2) * LANES:(h0 // 2 + 1) * LANES]
                x2 = jnp.concatenate([jnp.where(lo_half, xp, 0.0), jnp.where(lo_half, 0.0, xp)], axis=0).astype(BF16)
                yd = jnp.dot(m2, x2, preferred_element_type=F32)
                yc.append(yd + y_off[:, jp * LANES:(jp + 1) * LANES])
        y = jnp.concatenate(yc, axis=1) + xs * dsk_ref[...]
        y = y * _silu(z_ref[rs, :].astype(F32))
        yn = []
        for g in range(SSM_GROUPS):
            yg = y[:, g * 512:(g + 1) * 512]
            yn.append(yg * lax.rsqrt(jnp.mean(yg * yg, axis=-1, keepdims=True) + EPS))
        yn = jnp.concatenate(yn, axis=1) * sng_ref[...]
        o_ref[rs, GM_WIDTH + ATT_WIDTH:] = yn.astype(o_ref.dtype)

    kvprev_ref[...] = kv_ref[rows - CHUNK:rows, :]


def _mixers(p_uv, p_q, p_kv, p_z, p_xbc, p_dt, prm, bsz, seq):
    rows = 256
    steps = seq // rows
    t = bsz * seq

    def row_spec(width):
        return pl.BlockSpec((rows, width), lambda b, s: (b * steps + s, 0))

    def full_spec(shape):
        nd = len(shape)
        return pl.BlockSpec(shape, lambda b, s: (0,) * nd)

    consts = [prm["ln_g"], prm["ln_b"], prm["ws"], prm["bs_e"], prm["gm_g"], prm["att_g"],
              prm["conv_w"], prm["conv_b"], prm["dt_bias"], prm["a_log"], prm["dskip_e"], prm["ssm_g"], prm["e64"]]
    return pl.pallas_call(
        functools.partial(_mix_kernel, rows=rows),
        grid=(bsz, steps),
        in_specs=[pl.BlockSpec(memory_space=pltpu.SMEM),
                  row_spec(1024), row_spec(512), row_spec(256), row_spec(1024), row_spec(CONV_CH), row_spec(LANES)]
                 + [full_spec(a.shape) for a in consts],
        out_specs=row_spec(D_MODEL),
        out_shape=jax.ShapeDtypeStruct((t, D_MODEL), BF16),
        scratch_shapes=[pltpu.VMEM((CHUNK, 2 * KV_WIDTH), BF16),
                        pltpu.VMEM((rows + 8, CONV_CH), F32),
                        pltpu.VMEM((SSM_STATE, SSM_WIDTH), F32)],
        compiler_params=_cparams(("arbitrary", "arbitrary")),
        name="mixers",
    )(prm["sinks"], p_uv, p_q, p_kv, p_z, p_xbc, p_dt, *consts)


def _out_kernel(x_ref, m_ref, g1_ref, w_ref, o_ref):
    o_ref[...] = x_ref[...] + g1_ref[...] * jnp.dot(m_ref[...], w_ref[...], preferred_element_type=F32)


def _out_proj(x2, mix, mod_l, w_out_p, seq):
    t = x2.shape[0]
    tm = 512
    tiles_per_batch = seq // tm
    return pl.pallas_call(
        _out_kernel,
        grid=(t // tm,),
        in_specs=[
            pl.BlockSpec((tm, D_MODEL), lambda i: (i, 0)),
            pl.BlockSpec((tm, D_MODEL), lambda i: (i, 0)),
            _mod_spec(2, tiles_per_batch),
            pl.BlockSpec((D_MODEL, D_MODEL), lambda i: (0, 0), pipeline_mode=pl.Buffered(1)),
        ],
        out_specs=pl.BlockSpec((tm, D_MODEL), lambda i: (i, 0)),
        out_shape=jax.ShapeDtypeStruct((t, D_MODEL), F32),
        compiler_params=_cparams(("arbitrary",)),
        name="out_proj",
    )(x2, mix, mod_l, w_out_p)


def _mlp_kernel(x_ref, g_ref, sc_ref, sh_ref, g2_ref, w1_ref, w2_ref, fg_ref, o_ref, h_ref, acc_ref, *, final_norm):
    j = pl.program_id(1)

    @pl.when(j == 0)
    def _():
        x = x_ref[...]
        h = x * lax.rsqrt(jnp.mean(x * x, axis=-1, keepdims=True) + EPS) * g_ref[...]
        h_ref[...] = (h * (1.0 + sc_ref[...]) + sh_ref[...]).astype(BF16)

    a = jnp.dot(h_ref[...], w1_ref[...], preferred_element_type=F32)
    a = jnp.square(jnp.maximum(a, 0.0)).astype(BF16)
    contrib = jnp.dot(a, w2_ref[...], preferred_element_type=F32)

    @pl.when(j == 0)
    def _():
        acc_ref[...] = contrib

    @pl.when(j > 0)
    def _():
        acc_ref[...] += contrib

    @pl.when(j == pl.num_programs(1) - 1)
    def _():
        y = x_ref[...] + g2_ref[...] * acc_ref[...]
        if final_norm:
            y = y * lax.rsqrt(jnp.mean(y * y, axis=-1, keepdims=True) + EPS) * fg_ref[...]
        o_ref[...] = y


def _mlp(x2, mod_l, g, w1, w2, final_g, seq, final_norm):
    t = x2.shape[0]
    tm, tf = 512, 512
    tiles_per_batch = seq // tm

    def mspec(k):
        return pl.BlockSpec((None, None, 1, D_MODEL), lambda i, j: (i // tiles_per_batch, k, 0, 0))

    return pl.pallas_call(
        functools.partial(_mlp_kernel, final_norm=final_norm),
        grid=(t // tm, D_FF // tf),
        in_specs=[
            pl.BlockSpec((tm, D_MODEL), lambda i, j: (i, 0)),
            pl.BlockSpec((1, D_MODEL), lambda i, j: (0, 0)),
            mspec(4), mspec(3), mspec(5),
            pl.BlockSpec((D_MODEL, tf), lambda i, j: (0, j)),
            pl.BlockSpec((tf, D_MODEL), lambda i, j: (j, 0)),
            pl.BlockSpec((1, D_MODEL), lambda i, j: (0, 0)),
        ],
        out_specs=pl.BlockSpec((tm, D_MODEL), lambda i, j: (i, 0)),
        out_shape=jax.ShapeDtypeStruct((t, D_MODEL), F32),
        scratch_shapes=[pltpu.VMEM((tm, D_MODEL), BF16), pltpu.VMEM((tm, D_MODEL), F32)],
        compiler_params=_cparams(("arbitrary", "arbitrary")),
        name="mlp",
    )(x2, g.reshape(1, D_MODEL), mod_l, mod_l, mod_l, w1, w2, final_g.reshape(1, D_MODEL))


def _q_perm():
    idx = []
    for j in range(ATT_HEADS // 2):
        idx += list(range(j * ATT_HEAD_DIM, (j + 1) * ATT_HEAD_DIM))
        idx += list(range((4 + j) * ATT_HEAD_DIM, (5 + j) * ATT_HEAD_DIM))
    return np.asarray(idx, np.int32)


def _expand_matrix():
    e = np.zeros((LANES, SSM_WIDTH), np.float32)
    for piece in range(3):
        for h in range(SSM_HEADS):
            e[piece * SSM_HEADS + h, h * SSM_HEAD_DIM:(h + 1) * SSM_HEAD_DIM] = 1.0
    return jnp.asarray(e, BF16)


def _pad_lanes(v):
    return jnp.pad(v, (0, LANES - v.shape[0])).reshape(1, LANES)


def kernel(x, c, ada_w, ada_b, norm1_g, w_in, gm_ln_g, gm_ln_b, gm_ws, gm_bs, gm_norm_g, attn_sinks, attn_norm_g,
           conv_w, conv_b, dt_bias, a_log, d_skip, ssm_norm_g, w_out, norm2_g, w_mlp1, w_mlp2, final_norm_g):
    bsz, seq, d = x.shape
    n_layers = ada_w.shape[0]
    t = bsz * seq
    qperm = _q_perm()
    e64 = _expand_matrix()

    mod = _modulation(c, ada_w, ada_b).reshape(n_layers, bsz, 6, 1, d)
    x2 = x.reshape(t, d)
    for l in range(n_layers):
        mod_l = mod[l]
        w = w_in[l]
        w_in_p = jnp.concatenate(
            [w[:, :OFF_Q], w[:, OFF_Q:OFF_KV][:, qperm], w[:, OFF_KV:OFF_DT],
             jnp.pad(w[:, OFF_DT:], ((0, 0), (0, LANES - SSM_HEADS)))], axis=1).astype(BF16)
        wo = w_out[l]
        w_out_p = jnp.concatenate(
            [wo[:GM_WIDTH], wo[GM_WIDTH:GM_WIDTH + ATT_WIDTH][qperm], wo[GM_WIDTH + ATT_WIDTH:]], axis=0).astype(BF16)
        prm = {
            "sinks": attn_sinks[l],
            "ln_g": gm_ln_g[l].reshape(1, GM_WIDTH),
            "ln_b": gm_ln_b[l].reshape(1, GM_WIDTH),
            "ws": gm_ws[l],
            "bs_e": jnp.repeat(gm_bs[l].T, LANES, axis=1),
            "gm_g": gm_norm_g[l].reshape(1, GM_WIDTH),
            "att_g": attn_norm_g[l][qperm].reshape(1, ATT_WIDTH),
            "conv_w": conv_w[l],
            "conv_b": conv_b[l].reshape(1, CONV_CH),
            "dt_bias": _pad_lanes(dt_bias[l]),
            "a_log": _pad_lanes(a_log[l]),
            "dskip_e": jnp.repeat(d_skip[l], SSM_HEAD_DIM).reshape(1, SSM_WIDTH),
            "ssm_g": ssm_norm_g[l].reshape(1, SSM_WIDTH),
            "e64": e64,
        }
        p_uv, p_q, p_kv, p_z, p_xbc, p_dt = _in_proj(x2, mod_l, norm1_g[l], w_in_p, seq)
        mix = _mixers(p_uv, p_q, p_kv, p_z, p_xbc, p_dt, prm, bsz, seq)
        x2 = _out_proj(x2, mix, mod_l, w_out_p, seq)
        x2 = _mlp(x2, mod_l, norm2_g[l], w_mlp1[l].astype(BF16), w_mlp2[l].astype(BF16), final_norm_g, seq,
                  final_norm=(l == n_layers - 1))
    return x2.reshape(bsz, seq, d)
```

```python
import functools

import numpy as np
import jax
import jax.numpy as jnp
from jax import lax
from jax.experimental import pallas as pl
from jax.experimental.pallas import tpu as pltpu

F32 = jnp.float32
BF16 = jnp.bfloat16

D_MODEL = 2048
CHUNK = 128
GM_WIDTH = 512
GM_HEADS = 4
ATT_WIDTH = 512
ATT_HEADS = 8
ATT_HEAD_DIM = 64
KV_WIDTH = 128
SSM_WIDTH = 1024
SSM_HEADS = 16
SSM_HEAD_DIM = 64
SSM_GROUPS = 2
SSM_STATE = 128
CONV_WIDTH = 4
CONV_CH = 1536
D_FF = 4 * D_MODEL
NEG_INF = -1e30
EPS = 1e-6
LN_EPS = 1e-5
LANES = 128

OFF_UV, OFF_Q, OFF_KV, OFF_Z, OFF_XBC, OFF_DT = 0, 1024, 1536, 1792, 2816, 4352
IN_PAD = OFF_DT + LANES

VMEM_LIMIT = 56 * 1024 * 1024


def _cparams(sem):
    return pltpu.CompilerParams(dimension_semantics=sem, vmem_limit_bytes=VMEM_LIMIT)


def _silu(x):
    return x * jax.nn.sigmoid(x)


def _gelu_tanh(x):
    return 0.5 * x * (1.0 + jnp.tanh(0.7978845608028654 * (x + 0.044715 * (x * x * x))))


def _mod_kernel(c_ref, w_ref, b_ref, o_ref):
    ca = _silu(c_ref[...]).astype(BF16)
    o_ref[...] = jnp.dot(ca, w_ref[...].astype(BF16), preferred_element_type=F32) + b_ref[...]


def _modulation(c, ada_w, ada_b):
    n_layers, d, n = ada_w.shape
    bsz = c.shape[0]
    tn = 1024
    return pl.pallas_call(
        _mod_kernel,
        grid=(n_layers, n // tn),
        in_specs=[
            pl.BlockSpec((bsz, d), lambda l, j: (0, 0)),
            pl.BlockSpec((None, d, tn), lambda l, j: (l, 0, j)),
            pl.BlockSpec((None, 1, tn), lambda l, j: (l, 0, j)),
        ],
        out_specs=pl.BlockSpec((None, bsz, tn), lambda l, j: (l, 0, j)),
        out_shape=jax.ShapeDtypeStruct((n_layers, bsz, n), F32),
        compiler_params=_cparams(("arbitrary", "arbitrary")),
        name="modulation",
    )(c, ada_w, ada_b.reshape(n_layers, 1, n))


_IN_SEGS = ((OFF_UV, 1024), (OFF_Q, 512), (OFF_KV, 256), (OFF_Z, 1024), (OFF_XBC, 1536), (OFF_DT, LANES))


def _in_kernel(x_ref, g_ref, sc_ref, sh_ref, w_ref, o_uv, o_q, o_kv, o_z, o_xbc, o_dt):
    x = x_ref[...]
    h = x * lax.rsqrt(jnp.mean(x * x, axis=-1, keepdims=True) + EPS) * g_ref[...]
    h = h * (1.0 + sc_ref[...]) + sh_ref[...]
    hb = h.astype(BF16)
    for o_ref, (off, width) in zip((o_uv, o_q, o_kv, o_z, o_xbc, o_dt), _IN_SEGS):
        o_ref[...] = jnp.dot(hb, w_ref[:, off:off + width], preferred_element_type=F32).astype(o_ref.dtype)


def _mod_spec(k, rows_per_batch_tiles):
    return pl.BlockSpec((None, None, 1, D_MODEL), lambda i: (i // rows_per_batch_tiles, k, 0, 0))


def _in_proj(x2, mod_l, g, w_in_p, seq):
    t = x2.shape[0]
    tm = 512
    tiles_per_batch = seq // tm
    widths = [w for _, w in _IN_SEGS]
    dtypes = [BF16] * 5 + [F32]
    return pl.pallas_call(
        _in_kernel,
        grid=(t // tm,),
        in_specs=[
            pl.BlockSpec((tm, D_MODEL), lambda i: (i, 0)),
            pl.BlockSpec((1, D_MODEL), lambda i: (0, 0)),
            _mod_spec(1, tiles_per_batch),
            _mod_spec(0, tiles_per_batch),
            pl.BlockSpec((D_MODEL, IN_PAD), lambda i: (0, 0), pipeline_mode=pl.Buffered(1)),
        ],
        out_specs=[pl.BlockSpec((tm, w), lambda i: (i, 0)) for w in widths],
        out_shape=[jax.ShapeDtypeStruct((t, w), dt) for w, dt in zip(widths, dtypes)],
        compiler_params=_cparams(("arbitrary",)),
        name="in_proj",
    )(x2, g.reshape(1, D_MODEL), mod_l, mod_l, w_in_p)


def _split3_pack(x, lane):
    x = jnp.where(lane < SSM_HEADS, x, 0.0)
    hi = x.astype(BF16).astype(F32)
    r1 = x - hi
    mid = r1.astype(BF16).astype(F32)
    lo = r1 - mid
    packed = hi + pltpu.roll(mid, SSM_HEADS, axis=1) + pltpu.roll(lo, 2 * SSM_HEADS, axis=1)
    return packed.astype(BF16)


def _mix_kernel(sink_ref, uv_ref, q_ref, kv_ref, z_ref, xbc_ref, dt_ref,
                lng_ref, lnb_ref, ws_ref, bs_ref, gmg_ref, ang_ref,
                cw_ref, cb_ref, dtb_ref, alog_ref, dsk_ref, sng_ref, e64_ref,
                o_ref, kvprev_ref, ext_ref, state_ref, *, rows):
    s_idx = pl.program_id(1)
    nch = rows // CHUNK

    @pl.when(s_idx == 0)
    def _():
        kvprev_ref[...] = jnp.zeros_like(kvprev_ref)
        ext_ref[0:8, :] = jnp.zeros((8, CONV_CH), F32)
        state_ref[...] = jnp.zeros_like(state_ref)

    lane = lax.broadcasted_iota(jnp.int32, (CHUNK, LANES), 1)
    row = lax.broadcasted_iota(jnp.int32, (CHUNK, LANES), 0)
    causal = row >= lane
    lo_half = lane < ATT_HEAD_DIM
    tril_ones = jnp.where(causal, 1.0, 0.0).astype(BF16)

    ext_ref[8:8 + rows, :] = xbc_ref[...].astype(F32)
    acc = cb_ref[...] + cw_ref[0:1, :] * ext_ref[pl.ds(8 - 3, rows), :]
    for k in range(1, CONV_WIDTH):
        acc = acc + cw_ref[k:k + 1, :] * ext_ref[pl.ds(8 - 3 + k, rows), :]
    xconv = _silu(acc)
    ext_ref[0:8, :] = ext_ref[rows:rows + 8, :]

    a_neg = jnp.where(lane[0:1, :] < SSM_HEADS, -jnp.exp(alog_ref[...]), 0.0)
    e64 = e64_ref[...]

    qi = lax.broadcasted_iota(jnp.int32, (CHUNK, 2 * CHUNK), 0)
    kj = lax.broadcasted_iota(jnp.int32, (CHUNK, 2 * CHUNK), 1)
    diff = qi + CHUNK - kj
    band = (diff >= 0) & (diff < CHUNK)

    for c in range(nch):
        r0 = c * CHUNK
        rs = slice(r0, r0 + CHUNK)

        u = _gelu_tanh(uv_ref[rs, 0:GM_WIDTH].astype(F32))
        v = _gelu_tanh(uv_ref[rs, GM_WIDTH:2 * GM_WIDTH].astype(F32))
        ya = []
        for h in range(GM_HEADS):
            hs = slice(h * LANES, (h + 1) * LANES)
            vh = v[:, hs]
            mu = jnp.mean(vh, axis=-1, keepdims=True)
            xc = vh - mu
            var = jnp.mean(xc * xc, axis=-1, keepdims=True)
            vn = xc * lax.rsqrt(var + LN_EPS) * lng_ref[:, hs] + lnb_ref[:, hs]
            w = jnp.where(causal, ws_ref[h], 0.0).astype(BF16)
            gate = jnp.dot(w, vn.astype(BF16), preferred_element_type=F32) + bs_ref[:, hs]
            ya.append(u[:, hs] * gate)
        ya = jnp.concatenate(ya, axis=1)
        ya = ya * lax.rsqrt(jnp.mean(ya * ya, axis=-1, keepdims=True) + EPS) * gmg_ref[...]
        o_ref[rs, 0:GM_WIDTH] = ya.astype(o_ref.dtype)

        if c == 0:
            kprev = kvprev_ref[:, 0:KV_WIDTH]
            vprev = kvprev_ref[:, KV_WIDTH:2 * KV_WIDTH]
        else:
            kprev = kv_ref[r0 - CHUNK:r0, 0:KV_WIDTH]
            vprev = kv_ref[r0 - CHUNK:r0, KV_WIDTH:2 * KV_WIDTH]
        kcat = jnp.concatenate([kprev, kv_ref[rs, 0:KV_WIDTH]], axis=0)
        vcat = jnp.concatenate([vprev, kv_ref[rs, KV_WIDTH:2 * KV_WIDTH]], axis=0)
        blk = s_idx * nch + c
        valid = band & ((kj + (blk - 1) * CHUNK) >= 0)
        ob = []
        for j in range(ATT_HEADS // 2):
            qj = q_ref[rs, j * LANES:(j + 1) * LANES].astype(F32)
            q2 = jnp.concatenate([jnp.where(lo_half, qj, 0.0), jnp.where(lo_half, 0.0, qj)], axis=0).astype(BF16)
            s2 = lax.dot_general(q2, kcat, (((1,), (1,)), ((), ())), preferred_element_type=F32)
            s2 = s2 * (ATT_HEAD_DIM ** -0.5)
            halves = []
            for half in range(2):
                sink = sink_ref[j + 4 * half]
                sc = jnp.where(valid, s2[half * CHUNK:(half + 1) * CHUNK], NEG_INF)
                m = jnp.maximum(jnp.max(sc, axis=-1, keepdims=True), sink)
                e = jnp.exp(sc - m)
                den = jnp.sum(e, axis=-1, keepdims=True) + jnp.exp(sink - m)
                p = (e / den).astype(BF16)
                halves.append(jnp.dot(p, vcat, preferred_element_type=F32))
            ob.append(jnp.where(lo_half, halves[0], halves[1]))
        ob = jnp.concatenate(ob, axis=1)
        ob = ob * lax.rsqrt(jnp.mean(ob * ob, axis=-1, keepdims=True) + EPS) * ang_ref[...]
        o_ref[rs, GM_WIDTH:GM_WIDTH + ATT_WIDTH] = ob.astype(o_ref.dtype)

        xs = xconv[rs, 0:SSM_WIDTH]
        dt_in = dt_ref[rs, :] + dtb_ref[...]
        dt = jnp.maximum(dt_in, 0.0) + jnp.log1p(jnp.exp(-jnp.abs(dt_in)))
        da = dt * a_neg
        da_hi = da.astype(BF16)
        r1 = da - da_hi.astype(F32)
        da_mid = r1.astype(BF16)
        da_lo = (r1 - da_mid.astype(F32)).astype(BF16)
        a_cs = (jnp.dot(tril_ones, da_hi, preferred_element_type=F32)
                + jnp.dot(tril_ones, da_mid, preferred_element_type=F32)
                + jnp.dot(tril_ones, da_lo, preferred_element_type=F32))
        a_cs_t = a_cs.T
        ea = jnp.exp(a_cs)
        ds = jnp.exp(a_cs[CHUNK - 1:CHUNK, :] - a_cs)
        dt_e = jnp.dot(_split3_pack(dt, lane), e64, preferred_element_type=F32)
        ds_e = jnp.dot(_split3_pack(ds, lane), e64, preferred_element_type=F32)
        ea_e = jnp.dot(_split3_pack(ea, lane), e64, preferred_element_type=F32)
        xdt = xs * dt_e
        wst = (xdt * ds_e).astype(BF16)
        yc = []
        for g in range(SSM_GROUPS):
            gs = slice(g * 512, (g + 1) * 512)
            bg = xconv[rs, SSM_WIDTH + g * SSM_STATE:SSM_WIDTH + (g + 1) * SSM_STATE]
            cg = xconv[rs, SSM_WIDTH + 256 + g * SSM_STATE:SSM_WIDTH + 256 + (g + 1) * SSM_STATE]
            bgb = bg.astype(BF16)
            cgb = cg.astype(BF16)
            cb = lax.dot_general(cgb, bgb, (((1,), (1,)), ((), ())), preferred_element_type=F32)
            bgt = bg.T.astype(BF16)
            st_new = jnp.dot(bgt, wst[:, gs], preferred_element_type=F32)
            prev = state_ref[:, gs]
            y_off = jnp.dot(cgb, prev.astype(BF16), preferred_element_type=F32) * ea_e[:, gs]
            state_ref[:, gs] = prev * ea_e[CHUNK - 1:CHUNK, gs] + st_new
            for jp in range(4):
                h0 = g * 8 + 2 * jp
                ms = []
                for h in (h0, h0 + 1):
                    seg = a_cs[:, h:h + 1] - a_cs_t[h:h + 1, :]
                    ms.append(cb * jnp.exp(jnp.where(causal, seg, -jnp.inf)))
                m2 = jnp.concatenate(ms, axis=1).astype(BF16)
                xp = xdt[:, (h0 // 2) * LANES:(h0 // 2 + 1) * LANES]
                x2 = jnp.concatenate([jnp.where(lo_half, xp, 0.0), jnp.where(lo_half, 0.0, xp)], axis=0).astype(BF16)
                yd = jnp.dot(m2, x2, preferred_element_type=F32)
                yc.append(yd + y_off[:, jp * LANES:(jp + 1) * LANES])
        y = jnp.concatenate(yc, axis=1) + xs * dsk_ref[...]
        y = y * _silu(z_ref[rs, :].astype(F32))
        yn = []
        for g in range(SSM_GROUPS):
            yg = y[:, g * 512:(g + 1) * 512]
            yn.append(yg * lax.rsqrt(jnp.mean(yg * yg, axis=-1, keepdims=True) + EPS))
        yn = jnp.concatenate(yn, axis=1) * sng_ref[...]
        o_ref[rs, GM_WIDTH + ATT_WIDTH:] = yn.astype(o_ref.dtype)

    kvprev_ref[...] = kv_ref[rows - CHUNK:rows, :]


def _mixers(p_uv, p_q, p_kv, p_z, p_xbc, p_dt, prm, bsz, seq):
    rows = 256
    steps = seq // rows
    t = bsz * seq

    def row_spec(width):
        return pl.BlockSpec((rows, width), lambda b, s: (b * steps + s, 0))

    def full_spec(shape):
        nd = len(shape)
        return pl.BlockSpec(shape, lambda b, s: (0,) * nd)

    consts = [prm["ln_g"], prm["ln_b"], prm["ws"], prm["bs_e"], prm["gm_g"], prm["att_g"],
              prm["conv_w"], prm["conv_b"], prm["dt_bias"], prm["a_log"], prm["dskip_e"], prm["ssm_g"], prm["e64"]]
    return pl.pallas_call(
        functools.partial(_mix_kernel, rows=rows),
        grid=(bsz, steps),
        in_specs=[pl.BlockSpec(memory_space=pltpu.SMEM),
                  row_spec(1024), row_spec(512), row_spec(256), row_spec(1024), row_spec(CONV_CH), row_spec(LANES)]
                 + [full_spec(a.shape) for a in consts],
        out_specs=row_spec(D_MODEL),
        out_shape=jax.ShapeDtypeStruct((t, D_MODEL), BF16),
        scratch_shapes=[pltpu.VMEM((CHUNK, 2 * KV_WIDTH), BF16),
                        pltpu.VMEM((rows + 8, CONV_CH), F32),
                        pltpu.VMEM((SSM_STATE, SSM_WIDTH), F32)],
        compiler_params=_cparams(("arbitrary", "arbitrary")),
        name="mixers",
    )(prm["sinks"], p_uv, p_q, p_kv, p_z, p_xbc, p_dt, *consts)


def _out_kernel(x_ref, m_ref, g1_ref, w_ref, o_ref):
    o_ref[...] = x_ref[...] + g1_ref[...] * jnp.dot(m_ref[...], w_ref[...], preferred_element_type=F32)


def _out_proj(x2, mix, mod_l, w_out_p, seq):
    t = x2.shape[0]
    tm = 512
    tiles_per_batch = seq // tm
    return pl.pallas_call(
        _out_kernel,
        grid=(t // tm,),
        in_specs=[
            pl.BlockSpec((tm, D_MODEL), lambda i: (i, 0)),
            pl.BlockSpec((tm, D_MODEL), lambda i: (i, 0)),
            _mod_spec(2, tiles_per_batch),
            pl.BlockSpec((D_MODEL, D_MODEL), lambda i: (0, 0), pipeline_mode=pl.Buffered(1)),
        ],
        out_specs=pl.BlockSpec((tm, D_MODEL), lambda i: (i, 0)),
        out_shape=jax.ShapeDtypeStruct((t, D_MODEL), F32),
        compiler_params=_cparams(("arbitrary",)),
        name="out_proj",
    )(x2, mix, mod_l, w_out_p)


def _mlp_kernel(x_ref, g_ref, sc_ref, sh_ref, g2_ref, w1_ref, w2_ref, fg_ref, o_ref, h_ref, acc_ref, *, final_norm):
    j = pl.program_id(1)

    @pl.when(j == 0)
    def _():
        x = x_ref[...]
        h = x * lax.rsqrt(jnp.mean(x * x, axis=-1, keepdims=True) + EPS) * g_ref[...]
        h_ref[...] = (h * (1.0 + sc_ref[...]) + sh_ref[...]).astype(BF16)
        acc_ref[...] = jnp.zeros_like(acc_ref)

    a = jnp.dot(h_ref[...], w1_ref[...], preferred_element_type=F32)
    a = jnp.square(jnp.maximum(a, 0.0)).astype(BF16)
    acc_ref[...] += jnp.dot(a, w2_ref[...], preferred_element_type=F32)

    @pl.when(j == pl.num_programs(1) - 1)
    def _():
        y = x_ref[...] + g2_ref[...] * acc_ref[...]
        if final_norm:
            y = y * lax.rsqrt(jnp.mean(y * y, axis=-1, keepdims=True) + EPS) * fg_ref[...]
        o_ref[...] = y


def _mlp(x2, mod_l, g, w1, w2, final_g, seq, final_norm):
    t = x2.shape[0]
    tm, tf = 512, 1024
    tiles_per_batch = seq // tm

    def mspec(k):
        return pl.BlockSpec((None, None, 1, D_MODEL), lambda i, j: (i // tiles_per_batch, k, 0, 0))

    return pl.pallas_call(
        functools.partial(_mlp_kernel, final_norm=final_norm),
        grid=(t // tm, D_FF // tf),
        in_specs=[
            pl.BlockSpec((tm, D_MODEL), lambda i, j: (i, 0)),
            pl.BlockSpec((1, D_MODEL), lambda i, j: (0, 0)),
            mspec(4), mspec(3), mspec(5),
            pl.BlockSpec((D_MODEL, tf), lambda i, j: (0, j)),
            pl.BlockSpec((tf, D_MODEL), lambda i, j: (j, 0)),
            pl.BlockSpec((1, D_MODEL), lambda i, j: (0, 0)),
        ],
        out_specs=pl.BlockSpec((tm, D_MODEL), lambda i, j: (i, 0)),
        out_shape=jax.ShapeDtypeStruct((t, D_MODEL), F32),
        scratch_shapes=[pltpu.VMEM((tm, D_MODEL), BF16), pltpu.VMEM((tm, D_MODEL), F32)],
        compiler_params=_cparams(("arbitrary", "arbitrary")),
        name="mlp",
    )(x2, g.reshape(1, D_MODEL), mod_l, mod_l, mod_l, w1, w2, final_g.reshape(1, D_MODEL))


def _q_perm():
    idx = []
    for j in range(ATT_HEADS // 2):
        idx += list(range(j * ATT_HEAD_DIM, (j + 1) * ATT_HEAD_DIM))
        idx += list(range((4 + j) * ATT_HEAD_DIM, (5 + j) * ATT_HEAD_DIM))
    return np.asarray(idx, np.int32)


def _expand_matrix():
    e = np.zeros((LANES, SSM_WIDTH), np.float32)
    for piece in range(3):
        for h in range(SSM_HEADS):
            e[piece * SSM_HEADS + h, h * SSM_HEAD_DIM:(h + 1) * SSM_HEAD_DIM] = 1.0
    return jnp.asarray(e, BF16)


def _pad_lanes(v):
    return jnp.pad(v, (0, LANES - v.shape[0])).reshape(1, LANES)


def kernel(x, c, ada_w, ada_b, norm1_g, w_in, gm_ln_g, gm_ln_b, gm_ws, gm_bs, gm_norm_g, attn_sinks, attn_norm_g,
           conv_w, conv_b, dt_bias, a_log, d_skip, ssm_norm_g, w_out, norm2_g, w_mlp1, w_mlp2, final_norm_g):
    bsz, seq, d = x.shape
    n_layers = ada_w.shape[0]
    t = bsz * seq
    qperm = _q_perm()
    e64 = _expand_matrix()

    mod = _modulation(c, ada_w, ada_b).reshape(n_layers, bsz, 6, 1, d)
    x2 = x.reshape(t, d)
    for l in range(n_layers):
        mod_l = mod[l]
        w = w_in[l]
        w_in_p = jnp.concatenate(
            [w[:, :OFF_Q], w[:, OFF_Q:OFF_KV][:, qperm], w[:, OFF_KV:OFF_DT],
             jnp.pad(w[:, OFF_DT:], ((0, 0), (0, LANES - SSM_HEADS)))], axis=1).astype(BF16)
        wo = w_out[l]
        w_out_p = jnp.concatenate(
            [wo[:GM_WIDTH], wo[GM_WIDTH:GM_WIDTH + ATT_WIDTH][qperm], wo[GM_WIDTH + ATT_WIDTH:]], axis=0).astype(BF16)
        prm = {
            "sinks": attn_sinks[l],
            "ln_g": gm_ln_g[l].reshape(1, GM_WIDTH),
            "ln_b": gm_ln_b[l].reshape(1, GM_WIDTH),
            "ws": gm_ws[l],
            "bs_e": jnp.repeat(gm_bs[l].T, LANES, axis=1),
            "gm_g": gm_norm_g[l].reshape(1, GM_WIDTH),
            "att_g": attn_norm_g[l][qperm].reshape(1, ATT_WIDTH),
            "conv_w": conv_w[l],
            "conv_b": conv_b[l].reshape(1, CONV_CH),
            "dt_bias": _pad_lanes(dt_bias[l]),
            "a_log": _pad_lanes(a_log[l]),
            "dskip_e": jnp.repeat(d_skip[l], SSM_HEAD_DIM).reshape(1, SSM_WIDTH),
            "ssm_g": ssm_norm_g[l].reshape(1, SSM_WIDTH),
            "e64": e64,
        }
        p_uv, p_q, p_kv, p_z, p_xbc, p_dt = _in_proj(x2, mod_l, norm1_g[l], w_in_p, seq)
        mix = _mixers(p_uv, p_q, p_kv, p_z, p_xbc, p_dt, prm, bsz, seq)
        x2 = _out_proj(x2, mix, mod_l, w_out_p, seq)
        x2 = _mlp(x2, mod_l, norm2_g[l], w_mlp1[l].astype(BF16), w_mlp2[l].astype(BF16), final_norm_g, seq,
                  final_norm=(l == n_layers - 1))
    return x2.reshape(bsz, seq, d)
```

```python
import functools

import numpy as np
import jax
import jax.numpy as jnp
from jax import lax
from jax.experimental import pallas as pl
from jax.experimental.pallas import tpu as pltpu

F32 = jnp.float32
BF16 = jnp.bfloat16

D_MODEL = 2048
CHUNK = 128
GM_WIDTH = 512
GM_HEADS = 4
ATT_WIDTH = 512
ATT_HEADS = 8
ATT_HEAD_DIM = 64
KV_WIDTH = 128
SSM_WIDTH = 1024
SSM_HEADS = 16
SSM_HEAD_DIM = 64
SSM_GROUPS = 2
SSM_STATE = 128
CONV_WIDTH = 4
CONV_CH = 1536
D_FF = 4 * D_MODEL
NEG_INF = -1e30
EPS = 1e-6
LN_EPS = 1e-5
LANES = 128
SUBLANES = 8

OFF_UV, OFF_Q, OFF_KV, OFF_Z, OFF_XBC, OFF_DT = 0, 1024, 1536, 1792, 2816, 4352
IN_PAD = OFF_DT + LANES

VMEM_LIMIT = 56 * 1024 * 1024

MIX_ROWS = 256
PROJ_PIECE = 256
OUT_TM, OUT_SUB = 512, 256
MLP_TM, MLP_TF = 512, 1024


def _cparams(sem, flags=None):
    return pltpu.CompilerParams(dimension_semantics=sem, vmem_limit_bytes=VMEM_LIMIT, flags=flags)


def _silu(x):
    return x * jax.nn.sigmoid(x)


def _gelu_tanh(x):
    hx = 0.5 * x
    return hx + hx * jnp.tanh(x * (0.7978845608028654 + (0.7978845608028654 * 0.044715) * (x * x)))


def _rms(x):
    return x * lax.rsqrt(jnp.mean(x * x, axis=-1, keepdims=True) + EPS)


def _mod_kernel(c_ref, w_ref, b_ref, o_ref):
    ca = _silu(c_ref[...]).astype(BF16)
    o_ref[...] = jnp.dot(ca, w_ref[...].astype(BF16), preferred_element_type=F32) + b_ref[...]


def _modulation(c, ada_w, ada_b):
    n_layers, d, n = ada_w.shape
    bsz = c.shape[0]
    tn = 1024
    return pl.pallas_call(
        _mod_kernel,
        grid=(n_layers, n // tn),
        in_specs=[
            pl.BlockSpec((bsz, d), lambda l, j: (0, 0)),
            pl.BlockSpec((None, d, tn), lambda l, j: (l, 0, j)),
            pl.BlockSpec((None, 1, tn), lambda l, j: (l, 0, j)),
        ],
        out_specs=pl.BlockSpec((None, bsz, tn), lambda l, j: (l, 0, j)),
        out_shape=jax.ShapeDtypeStruct((n_layers, bsz, n), F32),
        compiler_params=_cparams(("arbitrary", "arbitrary")),
        name="modulation",
    )(c, ada_w, ada_b.reshape(n_layers, 1, n))


def _split3_pack(x, lane):
    x = jnp.where(lane < SSM_HEADS, x, 0.0)
    hi = x.astype(BF16).astype(F32)
    r1 = x - hi
    mid = r1.astype(BF16).astype(F32)
    lo = r1 - mid
    packed = hi + pltpu.roll(mid, SSM_HEADS, axis=1) + pltpu.roll(lo, 2 * SSM_HEADS, axis=1)
    return packed.astype(BF16)


def _proj_pieces(pw, rows):
    puv, pq, pkv, pz, pdt, ext = pw
    body = slice(SUBLANES, SUBLANES + rows)
    every = slice(0, rows)
    pieces = []
    for k in range(CONV_CH // PROJ_PIECE):
        pieces.append((ext, body, slice(k * PROJ_PIECE, (k + 1) * PROJ_PIECE), OFF_XBC + k * PROJ_PIECE, PROJ_PIECE))
    pieces.append((pdt, every, slice(0, LANES), OFF_DT, LANES))
    for k in range(2 * GM_WIDTH // PROJ_PIECE):
        pieces.append((puv, every, slice(k * PROJ_PIECE, (k + 1) * PROJ_PIECE), OFF_UV + k * PROJ_PIECE, PROJ_PIECE))
    pieces.append((pq, every, slice(0, ATT_WIDTH), OFF_Q, ATT_WIDTH))
    pieces.append((pkv, every, slice(0, 2 * KV_WIDTH), OFF_KV, 2 * KV_WIDTH))
    for k in range(SSM_WIDTH // PROJ_PIECE):
        pieces.append((pz, every, slice(k * PROJ_PIECE, (k + 1) * PROJ_PIECE), OFF_Z + k * PROJ_PIECE, PROJ_PIECE))
    return pieces


def _mix_stage(pw, pr, tile, sink_ref, x_ref, ng_ref, sc_ref, sh_ref, w_ref,
               lng_ref, lnb_ref, ws_ref, bs_ref, gmg_ref, ang_ref,
               cw_ref, cb_ref, dtb_ref, alog_ref, dsk_ref, sng_ref, e64_ref,
               o_ref, tail_ref, state_ref, *, rows, steps_per_batch):
    nch = rows // CHUNK
    puv_r, pq_r, pkv_r, pz_r, pdt_r, ext_r = pr
    blk0 = lax.rem(tile + steps_per_batch, steps_per_batch) * nch
    kv_before = jnp.where(blk0 > 0, pw[2][rows - CHUNK:rows, :].astype(F32), 0.0).astype(BF16)

    hb = ((_rms(x_ref[...]) * ng_ref[...]) * (1.0 + sc_ref[...]) + sh_ref[...]).astype(BF16)
    pieces = _proj_pieces(pw, rows)

    def emit_dots(n):
        for _ in range(2 * n):
            if pieces:
                dst, rws, cols, off, width = pieces.pop(0)
                dst[rws, cols] = jnp.dot(hb, w_ref[:, off:off + width], preferred_element_type=F32).astype(dst.dtype)

    lane = lax.broadcasted_iota(jnp.int32, (CHUNK, LANES), 1)
    row = lax.broadcasted_iota(jnp.int32, (CHUNK, LANES), 0)
    causal = row >= lane
    lo_half = lane < ATT_HEAD_DIM
    tril_ones = jnp.where(causal, 1.0, 0.0).astype(BF16)

    emit_dots(2)
    ext_r[0:SUBLANES, :] = tail_ref[...]
    ext = ext_r[...]
    acc = cb_ref[...] + cw_ref[CONV_WIDTH - 1:CONV_WIDTH, :] * ext[SUBLANES:SUBLANES + rows, :]
    for d in range(1, CONV_WIDTH):
        k = CONV_WIDTH - 1 - d
        acc = acc + cw_ref[k:k + 1, :] * pltpu.roll(ext, d, axis=0)[SUBLANES:SUBLANES + rows, :]
    xconv = _silu(acc)
    tail_ref[...] = ext[rows:rows + SUBLANES, :]

    a_neg = jnp.where(lane[0:1, :] < SSM_HEADS, -jnp.exp(alog_ref[...]), 0.0)
    e64 = e64_ref[...]

    qi = lax.broadcasted_iota(jnp.int32, (CHUNK, 2 * CHUNK), 0)
    kj = lax.broadcasted_iota(jnp.int32, (CHUNK, 2 * CHUNK), 1)
    diff = qi + CHUNK - kj
    band = (diff >= 0) & (diff < CHUNK)

    for c in range(nch):
        r0 = c * CHUNK
        rs = slice(r0, r0 + CHUNK)

        emit_dots(1)
        u = _gelu_tanh(puv_r[rs, 0:GM_WIDTH])
        v = _gelu_tanh(puv_r[rs, GM_WIDTH:2 * GM_WIDTH])
        ya = []
        for h in range(GM_HEADS):
            hs = slice(h * LANES, (h + 1) * LANES)
            vh = v[:, hs]
            mu = jnp.mean(vh, axis=-1, keepdims=True)
            xc = vh - mu
            var = jnp.mean(xc * xc, axis=-1, keepdims=True)
            vn = xc * lax.rsqrt(var + LN_EPS) * lng_ref[:, hs] + lnb_ref[:, hs]
            w = jnp.where(causal, ws_ref[h], 0.0).astype(BF16)
            gate = jnp.dot(w, vn.astype(BF16), preferred_element_type=F32) + bs_ref[:, hs]
            ya.append(u[:, hs] * gate)
        ya = _rms(jnp.concatenate(ya, axis=1)) * gmg_ref[...]
        o_ref[rs, 0:GM_WIDTH] = ya.astype(o_ref.dtype)

        emit_dots(1)
        if c == 0:
            kvp = kv_before
        else:
            kvp = pkv_r[r0 - CHUNK:r0, :]
        kvc = pkv_r[rs, :]
        kcat = jnp.concatenate([kvp[:, 0:KV_WIDTH], kvc[:, 0:KV_WIDTH]], axis=0)
        vcat = jnp.concatenate([kvp[:, KV_WIDTH:], kvc[:, KV_WIDTH:]], axis=0)
        valid = band & ((kj + (blk0 + c - 1) * CHUNK) >= 0)
        ob = []
        for j in range(ATT_HEADS // 2):
            qj = pq_r[rs, j * LANES:(j + 1) * LANES] * (ATT_HEAD_DIM ** -0.5)
            q2 = jnp.concatenate([jnp.where(lo_half, qj, 0.0), jnp.where(lo_half, 0.0, qj)], axis=0).astype(BF16)
            s2 = lax.dot_general(q2, kcat, (((1,), (1,)), ((), ())), preferred_element_type=F32)
            halves = []
            for half in range(2):
                sink = sink_ref[j + 4 * half]
                sc = jnp.where(valid, s2[half * CHUNK:(half + 1) * CHUNK], NEG_INF)
                m = jnp.maximum(jnp.max(sc, axis=-1, keepdims=True), sink)
                e = jnp.exp(sc - m)
                den = jnp.sum(e, axis=-1, keepdims=True) + jnp.exp(sink - m)
                halves.append(jnp.dot(e.astype(BF16), vcat, preferred_element_type=F32) / den)
            ob.append(jnp.where(lo_half, halves[0], halves[1]))
        ob = _rms(jnp.concatenate(ob, axis=1)) * ang_ref[...]
        o_ref[rs, GM_WIDTH:GM_WIDTH + ATT_WIDTH] = ob.astype(o_ref.dtype)

        emit_dots(1)
        xs = xconv[rs, 0:SSM_WIDTH]
        dt_in = pdt_r[rs, :] + dtb_ref[...]
        dt = jnp.maximum(dt_in, 0.0) + jnp.log(1.0 + jnp.exp(-jnp.abs(dt_in)))
        da = dt * a_neg
        da_hi = da.astype(BF16)
        r1 = da - da_hi.astype(F32)
        da_mid = r1.astype(BF16)
        da_lo = (r1 - da_mid.astype(F32)).astype(BF16)
        a_cs = (jnp.dot(tril_ones, da_hi, preferred_element_type=F32)
                + jnp.dot(tril_ones, da_mid, preferred_element_type=F32)
                + jnp.dot(tril_ones, da_lo, preferred_element_type=F32))
        a_cs_t = a_cs.T
        ea = jnp.exp(a_cs)
        ds = jnp.exp(a_cs[CHUNK - 1:CHUNK, :] - a_cs)
        dt_e = jnp.dot(_split3_pack(dt, lane), e64, preferred_element_type=F32)
        ds_e = jnp.dot(_split3_pack(ds, lane), e64, preferred_element_type=F32)
        ea_e = jnp.dot(_split3_pack(ea, lane), e64, preferred_element_type=F32)
        xdt = xs * dt_e
        wst = (xdt * ds_e).astype(BF16)
        yc = []
        for g in range(SSM_GROUPS):
            emit_dots(1)
            gs = slice(g * 512, (g + 1) * 512)
            bg = xconv[rs, SSM_WIDTH + g * SSM_STATE:SSM_WIDTH + (g + 1) * SSM_STATE]
            cg = xconv[rs, SSM_WIDTH + 256 + g * SSM_STATE:SSM_WIDTH + 256 + (g + 1) * SSM_STATE]
            bgb = bg.astype(BF16)
            cgb = cg.astype(BF16)
            cb = lax.dot_general(cgb, bgb, (((1,), (1,)), ((), ())), preferred_element_type=F32)
            bgt = bg.T.astype(BF16)
            st_new = jnp.dot(bgt, wst[:, gs], preferred_element_type=F32)
            prev = state_ref[:, gs]
            y_off = jnp.dot(cgb, prev.astype(BF16), preferred_element_type=F32) * ea_e[:, gs]
            state_ref[:, gs] = prev * ea_e[CHUNK - 1:CHUNK, gs] + st_new
            for jp in range(4):
                h0 = g * 8 + 2 * jp
                ms = []
                for h in (h0, h0 + 1):
                    seg = a_cs[:, h:h + 1] - a_cs_t[h:h + 1, :]
                    ms.append(cb * jnp.exp(jnp.where(causal, seg, -jnp.inf)))
                m2 = jnp.concatenate(ms, axis=1).astype(BF16)
                xp = xdt[:, (h0 // 2) * LANES:(h0 // 2 + 1) * LANES]
                x2 = jnp.concatenate([jnp.where(lo_half, xp, 0.0), jnp.where(lo_half, 0.0, xp)], axis=0).astype(BF16)
                yd = jnp.dot(m2, x2, preferred_element_type=F32)
                yc.append(yd + y_off[:, jp * LANES:(jp + 1) * LANES])
        y = jnp.concatenate(yc, axis=1) + xs * dsk_ref[...]
        y = y * _silu(pz_r[rs, :])
        yn = jnp.concatenate([_rms(y[:, g * 512:(g + 1) * 512]) for g in range(SSM_GROUPS)], axis=1) * sng_ref[...]
        o_ref[rs, GM_WIDTH + ATT_WIDTH:] = yn.astype(o_ref.dtype)

    emit_dots(len(pieces))


def _mix_kernel(*refs, rows, steps_per_batch):
    n_in = 19
    ins, o_ref, scratch = refs[:n_in], refs[n_in], refs[n_in + 1:]
    buf_a, buf_b = scratch[0:6], scratch[6:12]
    tail_ref, state_ref = scratch[12:14]
    i = pl.program_id(0)
    tile = i - 1

    @pl.when(i == 0)
    def _():
        for r in buf_a + buf_b:
            r[...] = jnp.zeros_like(r)

    @pl.when((i == 0) | (lax.rem(tile + steps_per_batch, steps_per_batch) == 0))
    def _():
        tail_ref[...] = jnp.zeros_like(tail_ref)
        state_ref[...] = jnp.zeros_like(state_ref)

    stage = functools.partial(_mix_stage, rows=rows, steps_per_batch=steps_per_batch)

    @pl.when(lax.rem(i, 2) == 0)
    def _():
        stage(buf_a, buf_b, tile, *ins, o_ref, tail_ref, state_ref)

    @pl.when(lax.rem(i, 2) == 1)
    def _():
        stage(buf_b, buf_a, tile, *ins, o_ref, tail_ref, state_ref)


def _mix_layer(x2, mod_l, norm1_g, w_in_p, prm, bsz, seq):
    rows = MIX_ROWS
    steps = seq // rows
    n_tiles = bsz * steps
    t = bsz * seq
    x_spec = pl.BlockSpec((rows, D_MODEL), lambda i: (jnp.minimum(i, n_tiles - 1), 0))
    o_spec = pl.BlockSpec((rows, D_MODEL), lambda i: (jnp.maximum(i - 1, 0), 0))

    def full_spec(shape):
        nd = len(shape)
        return pl.BlockSpec(shape, lambda i: (0,) * nd)

    def mod_spec(k):
        return pl.BlockSpec((None, None, 1, D_MODEL), lambda i: (jnp.minimum(i, n_tiles - 1) // steps, k, 0, 0))

    consts = [prm["ln_g"], prm["ln_b"], prm["ws"], prm["bs_e"], prm["gm_g"], prm["att_g"],
              prm["conv_w"], prm["conv_b"], prm["dt_bias"], prm["a_log"], prm["dskip_e"], prm["ssm_g"], prm["e64"]]

    def p_buffers():
        return [pltpu.VMEM((rows, 2 * GM_WIDTH), F32), pltpu.VMEM((rows, ATT_WIDTH), F32),
                pltpu.VMEM((rows, 2 * KV_WIDTH), BF16), pltpu.VMEM((rows, SSM_WIDTH), F32),
                pltpu.VMEM((rows, LANES), F32), pltpu.VMEM((rows + SUBLANES, CONV_CH), F32)]

    return pl.pallas_call(
        functools.partial(_mix_kernel, rows=rows, steps_per_batch=steps),
        grid=(n_tiles + 1,),
        in_specs=[pl.BlockSpec(memory_space=pltpu.SMEM),
                  x_spec, full_spec((1, D_MODEL)), mod_spec(1), mod_spec(0),
                  pl.BlockSpec((D_MODEL, IN_PAD), lambda i: (0, 0), pipeline_mode=pl.Buffered(1))]
                 + [full_spec(a.shape) for a in consts],
        out_specs=o_spec,
        out_shape=jax.ShapeDtypeStruct((t, D_MODEL), BF16),
        scratch_shapes=p_buffers() + p_buffers()
                       + [pltpu.VMEM((SUBLANES, CONV_CH), F32),
                          pltpu.VMEM((SSM_STATE, SSM_WIDTH), F32)],
        compiler_params=_cparams(("arbitrary",)),
        name="mix_layer",
    )(prm["sinks"], x2, norm1_g.reshape(1, D_MODEL), mod_l, mod_l, w_in_p, *consts)


def _out_kernel(x_ref, m_ref, g1_ref, w_ref, ng_ref, sc_ref, sh_ref, o_ref, h_ref, *, sub):
    for r in range(x_ref.shape[0] // sub):
        rs = slice(r * sub, (r + 1) * sub)
        y = x_ref[rs, :] + g1_ref[...] * jnp.dot(m_ref[rs, :], w_ref[...], preferred_element_type=F32)
        o_ref[rs, :] = y
        h_ref[rs, :] = ((_rms(y) * ng_ref[...]) * (1.0 + sc_ref[...]) + sh_ref[...]).astype(BF16)


def _out_proj(x2, mix, mod_l, w_out_p, norm2_g, seq):
    t = x2.shape[0]
    tm = OUT_TM
    tiles_per_batch = seq // tm
    row = pl.BlockSpec((tm, D_MODEL), lambda i: (i, 0))
    vec = pl.BlockSpec((1, D_MODEL), lambda i: (0, 0))

    def mod_spec(k):
        return pl.BlockSpec((None, None, 1, D_MODEL), lambda i: (i // tiles_per_batch, k, 0, 0))

    return pl.pallas_call(
        functools.partial(_out_kernel, sub=OUT_SUB),
        grid=(t // tm,),
        in_specs=[
            row, row, mod_spec(2),
            pl.BlockSpec((D_MODEL, D_MODEL), lambda i: (0, 0), pipeline_mode=pl.Buffered(1)),
            vec, mod_spec(4), mod_spec(3),
        ],
        out_specs=[row, row],
        out_shape=[jax.ShapeDtypeStruct((t, D_MODEL), F32), jax.ShapeDtypeStruct((t, D_MODEL), BF16)],
        compiler_params=_cparams(("arbitrary",)),
        name="out_proj",
    )(x2, mix, mod_l, w_out_p, norm2_g.reshape(1, D_MODEL), mod_l, mod_l)


def _mlp_kernel(x_ref, h_ref, g2_ref, w1_ref, w2_ref, fg_ref, o_ref, acc_ref, *, final_norm):
    j = pl.program_id(1)

    @pl.when(j == 0)
    def _():
        acc_ref[...] = jnp.zeros_like(acc_ref)

    a = jnp.dot(h_ref[...], w1_ref[...], preferred_element_type=F32)
    a = jnp.square(jnp.maximum(a, 0.0)).astype(BF16)
    acc_ref[...] += jnp.dot(a, w2_ref[...], preferred_element_type=F32)

    @pl.when(j == pl.num_programs(1) - 1)
    def _():
        y = x_ref[...] + g2_ref[...] * acc_ref[...]
        if final_norm:
            y = _rms(y) * fg_ref[...]
        o_ref[...] = y


def _mlp(x2, h2, mod_l, w1, w2, final_g, seq, final_norm):
    t = x2.shape[0]
    tm, tf = MLP_TM, MLP_TF
    tiles_per_batch = seq // tm
    row = pl.BlockSpec((tm, D_MODEL), lambda i, j: (i, 0))
    return pl.pallas_call(
        functools.partial(_mlp_kernel, final_norm=final_norm),
        grid=(t // tm, D_FF // tf),
        in_specs=[
            row, row,
            pl.BlockSpec((None, None, 1, D_MODEL), lambda i, j: (i // tiles_per_batch, 5, 0, 0)),
            pl.BlockSpec((D_MODEL, tf), lambda i, j: (0, j)),
            pl.BlockSpec((tf, D_MODEL), lambda i, j: (j, 0)),
            pl.BlockSpec((1, D_MODEL), lambda i, j: (0, 0)),
        ],
        out_specs=row,
        out_shape=jax.ShapeDtypeStruct((t, D_MODEL), F32),
        scratch_shapes=[pltpu.VMEM((tm, D_MODEL), F32)],
        compiler_params=_cparams(("arbitrary", "arbitrary")),
        name="mlp",
    )(x2, h2, mod_l, w1, w2, final_g.reshape(1, D_MODEL))


def _q_perm():
    idx = []
    for j in range(ATT_HEADS // 2):
        idx += list(range(j * ATT_HEAD_DIM, (j + 1) * ATT_HEAD_DIM))
        idx += list(range((4 + j) * ATT_HEAD_DIM, (5 + j) * ATT_HEAD_DIM))
    return np.asarray(idx, np.int32)


def _expand_matrix():
    e = np.zeros((LANES, SSM_WIDTH), np.float32)
    for piece in range(3):
        for h in range(SSM_HEADS):
            e[piece * SSM_HEADS + h, h * SSM_HEAD_DIM:(h + 1) * SSM_HEAD_DIM] = 1.0
    return jnp.asarray(e, BF16)


def _pad_lanes(v):
    return jnp.pad(v, (0, LANES - v.shape[0])).reshape(1, LANES)


def kernel(x, c, ada_w, ada_b, norm1_g, w_in, gm_ln_g, gm_ln_b, gm_ws, gm_bs, gm_norm_g, attn_sinks, attn_norm_g,
           conv_w, conv_b, dt_bias, a_log, d_skip, ssm_norm_g, w_out, norm2_g, w_mlp1, w_mlp2, final_norm_g):
    bsz, seq, d = x.shape
    n_layers = ada_w.shape[0]
    t = bsz * seq
    qperm = _q_perm()
    e64 = _expand_matrix()

    mod = _modulation(c, ada_w, ada_b).reshape(n_layers, bsz, 6, 1, d)
    x2 = x.reshape(t, d)
    for l in range(n_layers):
        mod_l = mod[l]
        w = w_in[l]
        w_in_p = jnp.concatenate(
            [w[:, :OFF_Q], w[:, OFF_Q:OFF_KV][:, qperm], w[:, OFF_KV:OFF_DT],
             jnp.pad(w[:, OFF_DT:], ((0, 0), (0, LANES - SSM_HEADS)))], axis=1).astype(BF16)
        wo = w_out[l]
        w_out_p = jnp.concatenate(
            [wo[:GM_WIDTH], wo[GM_WIDTH:GM_WIDTH + ATT_WIDTH][qperm], wo[GM_WIDTH + ATT_WIDTH:]], axis=0).astype(BF16)
        prm = {
            "sinks": attn_sinks[l],
            "ln_g": gm_ln_g[l].reshape(1, GM_WIDTH),
            "ln_b": gm_ln_b[l].reshape(1, GM_WIDTH),
            "ws": gm_ws[l],
            "bs_e": jnp.repeat(gm_bs[l].T, LANES, axis=1),
            "gm_g": gm_norm_g[l].reshape(1, GM_WIDTH),
            "att_g": attn_norm_g[l][qperm].reshape(1, ATT_WIDTH),
            "conv_w": conv_w[l],
            "conv_b": conv_b[l].reshape(1, CONV_CH),
            "dt_bias": _pad_lanes(dt_bias[l]),
            "a_log": _pad_lanes(a_log[l]),
            "dskip_e": jnp.repeat(d_skip[l], SSM_HEAD_DIM).reshape(1, SSM_WIDTH),
            "ssm_g": ssm_norm_g[l].reshape(1, SSM_WIDTH),
            "e64": e64,
        }
        mix = _mix_layer(x2, mod_l, norm1_g[l], w_in_p, prm, bsz, seq)
        x2, h2 = _out_proj(x2, mix, mod_l, w_out_p, norm2_g[l], seq)
        x2 = _mlp(x2, h2, mod_l, w_mlp1[l].astype(BF16), w_mlp2[l].astype(BF16), final_norm_g, seq,
                  final_norm=(l == n_layers - 1))
    return x2.reshape(bsz, seq, d)
```

```python
import functools

import numpy as np
import jax
import jax.numpy as jnp
from jax import lax
from jax.experimental import pallas as pl
from jax.experimental.pallas import tpu as pltpu

F32 = jnp.float32
BF16 = jnp.bfloat16

D_MODEL = 2048
CHUNK = 128
GM_WIDTH = 512
GM_HEADS = 4
ATT_WIDTH = 512
ATT_HEADS = 8
ATT_HEAD_DIM = 64
KV_WIDTH = 128
SSM_WIDTH = 1024
SSM_HEADS = 16
SSM_HEAD_DIM = 64
SSM_GROUPS = 2
SSM_STATE = 128
CONV_WIDTH = 4
CONV_CH = 1536
D_FF = 4 * D_MODEL
NEG_INF = -1e30
EPS = 1e-6
LN_EPS = 1e-5
LANES = 128
SUBLANES = 8

OFF_UV, OFF_Q, OFF_KV, OFF_Z, OFF_XBC, OFF_DT = 0, 1024, 1536, 1792, 2816, 4352
IN_PAD = OFF_DT + LANES

VMEM_LIMIT = 56 * 1024 * 1024
MLP_VMEM_LIMIT = 60 * 1024 * 1024

MIX_ROWS = 256
PROJ_PIECE = 256
EMIT = (3, 2, 2, 1, 1)
OUT_TM, OUT_SUB = 512, 256
MLP_TM, MLP_TF = 1024, 512

def _cparams(sem, vmem_limit=VMEM_LIMIT):
    return pltpu.CompilerParams(dimension_semantics=sem, vmem_limit_bytes=vmem_limit)


def _silu(x):
    return x * jax.nn.sigmoid(x)


def _gelu_tanh(x):
    hx = 0.5 * x
    return hx + hx * jnp.tanh(x * (0.7978845608028654 + (0.7978845608028654 * 0.044715) * (x * x)))


def _rms(x):
    return x * lax.rsqrt(jnp.mean(x * x, axis=-1, keepdims=True) + EPS)


def _mod_kernel(c_ref, w_ref, b_ref, o_ref):
    ca = _silu(c_ref[...]).astype(BF16)
    o_ref[...] = jnp.dot(ca, w_ref[...].astype(BF16), preferred_element_type=F32) + b_ref[...]


def _modulation(c, ada_w, ada_b):
    n_layers, d, n = ada_w.shape
    bsz = c.shape[0]
    tn = 1024
    return pl.pallas_call(
        _mod_kernel,
        grid=(n_layers, n // tn),
        in_specs=[
            pl.BlockSpec((bsz, d), lambda l, j: (0, 0)),
            pl.BlockSpec((None, d, tn), lambda l, j: (l, 0, j)),
            pl.BlockSpec((None, 1, tn), lambda l, j: (l, 0, j)),
        ],
        out_specs=pl.BlockSpec((None, bsz, tn), lambda l, j: (l, 0, j)),
        out_shape=jax.ShapeDtypeStruct((n_layers, bsz, n), F32),
        compiler_params=_cparams(("arbitrary", "arbitrary")),
        name="modulation",
    )(c, ada_w, ada_b.reshape(n_layers, 1, n))


def _split3_pack(x, lane):
    x = jnp.where(lane < SSM_HEADS, x, 0.0)
    hi = x.astype(BF16).astype(F32)
    r1 = x - hi
    mid = r1.astype(BF16).astype(F32)
    lo = r1 - mid
    packed = hi + pltpu.roll(mid, SSM_HEADS, axis=1) + pltpu.roll(lo, 2 * SSM_HEADS, axis=1)
    return packed.astype(BF16)


def _proj_pieces(pw, rows):
    puv, pq, pkv, pz, pdt, ext = pw
    body = slice(SUBLANES, SUBLANES + rows)
    every = slice(0, rows)
    pieces = []
    for k in range(CONV_CH // PROJ_PIECE):
        pieces.append((ext, body, slice(k * PROJ_PIECE, (k + 1) * PROJ_PIECE), OFF_XBC + k * PROJ_PIECE, PROJ_PIECE))
    pieces.append((pdt, every, slice(0, LANES), OFF_DT, LANES))
    for k in range(2 * GM_WIDTH // PROJ_PIECE):
        pieces.append((puv, every, slice(k * PROJ_PIECE, (k + 1) * PROJ_PIECE), OFF_UV + k * PROJ_PIECE, PROJ_PIECE))
    pieces.append((pq, every, slice(0, ATT_WIDTH), OFF_Q, ATT_WIDTH))
    pieces.append((pkv, every, slice(0, 2 * KV_WIDTH), OFF_KV, 2 * KV_WIDTH))
    for k in range(SSM_WIDTH // PROJ_PIECE):
        pieces.append((pz, every, slice(k * PROJ_PIECE, (k + 1) * PROJ_PIECE), OFF_Z + k * PROJ_PIECE, PROJ_PIECE))
    return pieces


def _mix_stage(pw, pr, tile, sink_ref, x_ref, ng_ref, sc_ref, sh_ref, w_ref,
               lng_ref, lnb_ref, ws_ref, bs_ref, gmg_ref, ang_ref,
               cw_ref, cb_ref, dtb_ref, alog_ref, dsk_ref, sng_ref, e64_ref,
               o_ref, tail_ref, state_ref, *, rows, steps_per_batch):
    nch = rows // CHUNK
    puv_r, pq_r, pkv_r, pz_r, pdt_r, ext_r = pr
    blk0 = lax.rem(tile + steps_per_batch, steps_per_batch) * nch
    kv_before = jnp.where(blk0 > 0, pw[2][rows - CHUNK:rows, :].astype(F32), 0.0).astype(BF16)

    hb = ((_rms(x_ref[...]) * ng_ref[...]) * (1.0 + sc_ref[...]) + sh_ref[...]).astype(BF16)
    pieces = _proj_pieces(pw, rows)

    def emit_dots(n):
        for _ in range(n):
            if pieces:
                dst, rws, cols, off, width = pieces.pop(0)
                dst[rws, cols] = jnp.dot(hb, w_ref[:, off:off + width], preferred_element_type=F32).astype(dst.dtype)

    lane = lax.broadcasted_iota(jnp.int32, (CHUNK, LANES), 1)
    row = lax.broadcasted_iota(jnp.int32, (CHUNK, LANES), 0)
    causal = row >= lane
    lo_half = lane < ATT_HEAD_DIM
    tril_ones = jnp.where(causal, 1.0, 0.0).astype(BF16)

    emit_dots(EMIT[0])
    ext_r[0:SUBLANES, :] = tail_ref[...]
    ext = ext_r[...]
    acc = cb_ref[...] + cw_ref[CONV_WIDTH - 1:CONV_WIDTH, :] * ext[SUBLANES:SUBLANES + rows, :]
    for d in range(1, CONV_WIDTH):
        k = CONV_WIDTH - 1 - d
        acc = acc + cw_ref[k:k + 1, :] * pltpu.roll(ext, d, axis=0)[SUBLANES:SUBLANES + rows, :]
    xconv = _silu(acc)
    tail_ref[...] = ext[rows:rows + SUBLANES, :]

    a_neg = jnp.where(lane[0:1, :] < SSM_HEADS, -jnp.exp(alog_ref[...]), 0.0)
    e64 = e64_ref[...]

    qi = lax.broadcasted_iota(jnp.int32, (CHUNK, 2 * CHUNK), 0)
    kj = lax.broadcasted_iota(jnp.int32, (CHUNK, 2 * CHUNK), 1)
    diff = qi + CHUNK - kj
    band = (diff >= 0) & (diff < CHUNK)

    for c in range(nch):
        r0 = c * CHUNK
        rs = slice(r0, r0 + CHUNK)

        emit_dots(EMIT[1])
        u = _gelu_tanh(puv_r[rs, 0:GM_WIDTH])
        v = _gelu_tanh(puv_r[rs, GM_WIDTH:2 * GM_WIDTH])
        ya = []
        for h in range(GM_HEADS):
            hs = slice(h * LANES, (h + 1) * LANES)
            vh = v[:, hs]
            mu = jnp.mean(vh, axis=-1, keepdims=True)
            xc = vh - mu
            var = jnp.mean(xc * xc, axis=-1, keepdims=True)
            vn = xc * lax.rsqrt(var + LN_EPS) * lng_ref[:, hs] + lnb_ref[:, hs]
            w = jnp.where(causal, ws_ref[h], 0.0).astype(BF16)
            gate = jnp.dot(w, vn.astype(BF16), preferred_element_type=F32) + bs_ref[:, hs]
            ya.append(u[:, hs] * gate)
        ya = _rms(jnp.concatenate(ya, axis=1)) * gmg_ref[...]
        o_ref[rs, 0:GM_WIDTH] = ya.astype(o_ref.dtype)

        emit_dots(EMIT[2])
        if c == 0:
            kvp = kv_before
        else:
            kvp = pkv_r[r0 - CHUNK:r0, :]
        kvc = pkv_r[rs, :]
        kcat = jnp.concatenate([kvp[:, 0:KV_WIDTH], kvc[:, 0:KV_WIDTH]], axis=0)
        vcat = jnp.concatenate([kvp[:, KV_WIDTH:], kvc[:, KV_WIDTH:]], axis=0)
        valid = band & ((kj + (blk0 + c - 1) * CHUNK) >= 0)
        ob = []
        for j in range(ATT_HEADS // 2):
            qj = pq_r[rs, j * LANES:(j + 1) * LANES] * (ATT_HEAD_DIM ** -0.5)
            q2 = jnp.concatenate([jnp.where(lo_half, qj, 0.0), jnp.where(lo_half, 0.0, qj)], axis=0).astype(BF16)
            s2 = lax.dot_general(q2, kcat, (((1,), (1,)), ((), ())), preferred_element_type=F32)
            halves = []
            for half in range(2):
                sink = sink_ref[j + 4 * half]
                sc = jnp.where(valid, s2[half * CHUNK:(half + 1) * CHUNK], NEG_INF)
                m = jnp.maximum(jnp.max(sc, axis=-1, keepdims=True), sink)
                e = jnp.exp(sc - m)
                den = jnp.sum(e, axis=-1, keepdims=True) + jnp.exp(sink - m)
                halves.append(jnp.dot(e.astype(BF16), vcat, preferred_element_type=F32) / den)
            ob.append(jnp.where(lo_half, halves[0], halves[1]))
        ob = _rms(jnp.concatenate(ob, axis=1)) * ang_ref[...]
        o_ref[rs, GM_WIDTH:GM_WIDTH + ATT_WIDTH] = ob.astype(o_ref.dtype)

        emit_dots(EMIT[3])
        xs = xconv[rs, 0:SSM_WIDTH]
        dt_in = pdt_r[rs, :] + dtb_ref[...]
        dt = jnp.maximum(dt_in, 0.0) + jnp.log(1.0 + jnp.exp(-jnp.abs(dt_in)))
        da = dt * a_neg
        da_hi = da.astype(BF16)
        r1 = da - da_hi.astype(F32)
        da_mid = r1.astype(BF16)
        da_lo = (r1 - da_mid.astype(F32)).astype(BF16)
        a_cs = (jnp.dot(tril_ones, da_hi, preferred_element_type=F32)
                + jnp.dot(tril_ones, da_mid, preferred_element_type=F32)
                + jnp.dot(tril_ones, da_lo, preferred_element_type=F32))
        a_cs_t = a_cs.T
        ea = jnp.exp(a_cs)
        ds = jnp.exp(a_cs[CHUNK - 1:CHUNK, :] - a_cs)
        dt_e = jnp.dot(_split3_pack(dt, lane), e64, preferred_element_type=F32)
        ds_e = jnp.dot(_split3_pack(ds, lane), e64, preferred_element_type=F32)
        ea_e = jnp.dot(_split3_pack(ea, lane), e64, preferred_element_type=F32)
        xdt = xs * dt_e
        wst = (xdt * ds_e).astype(BF16)
        yc = []
        for g in range(SSM_GROUPS):
            emit_dots(EMIT[4])
            gs = slice(g * 512, (g + 1) * 512)
            bg = xconv[rs, SSM_WIDTH + g * SSM_STATE:SSM_WIDTH + (g + 1) * SSM_STATE]
            cg = xconv[rs, SSM_WIDTH + 256 + g * SSM_STATE:SSM_WIDTH + 256 + (g + 1) * SSM_STATE]
            bgb = bg.astype(BF16)
            cgb = cg.astype(BF16)
            cb = lax.dot_general(cgb, bgb, (((1,), (1,)), ((), ())), preferred_element_type=F32)
            bgt = bg.T.astype(BF16)
            st_new = jnp.dot(bgt, wst[:, gs], preferred_element_type=F32)
            prev = state_ref[:, gs]
            y_off = jnp.dot(cgb, prev.astype(BF16), preferred_element_type=F32) * ea_e[:, gs]
            state_ref[:, gs] = prev * ea_e[CHUNK - 1:CHUNK, gs] + st_new
            for jp in range(4):
                h0 = g * 8 + 2 * jp
                ms = []
                for h in (h0, h0 + 1):
                    seg = a_cs[:, h:h + 1] - a_cs_t[h:h + 1, :]
                    ms.append(cb * jnp.exp(jnp.where(causal, seg, -jnp.inf)))
                m2 = jnp.concatenate(ms, axis=1).astype(BF16)
                xp = xdt[:, (h0 // 2) * LANES:(h0 // 2 + 1) * LANES]
                x2 = jnp.concatenate([jnp.where(lo_half, xp, 0.0), jnp.where(lo_half, 0.0, xp)], axis=0).astype(BF16)
                yd = jnp.dot(m2, x2, preferred_element_type=F32)
                yc.append(yd + y_off[:, jp * LANES:(jp + 1) * LANES])
        y = jnp.concatenate(yc, axis=1) + xs * dsk_ref[...]
        y = y * _silu(pz_r[rs, :])
        yn = jnp.concatenate([_rms(y[:, g * 512:(g + 1) * 512]) for g in range(SSM_GROUPS)], axis=1) * sng_ref[...]
        o_ref[rs, GM_WIDTH + ATT_WIDTH:] = yn.astype(o_ref.dtype)

    emit_dots(len(pieces))


def _mix_kernel(*refs, rows, steps_per_batch):
    n_in = 19
    ins, o_ref, scratch = refs[:n_in], refs[n_in], refs[n_in + 1:]
    buf_a, buf_b = scratch[0:6], scratch[6:12]
    tail_ref, state_ref = scratch[12:14]
    i = pl.program_id(0)
    tile = i - 1

    @pl.when(i == 0)
    def _():
        for r in buf_a + buf_b:
            r[...] = jnp.zeros_like(r)

    @pl.when((i == 0) | (lax.rem(tile + steps_per_batch, steps_per_batch) == 0))
    def _():
        tail_ref[...] = jnp.zeros_like(tail_ref)
        state_ref[...] = jnp.zeros_like(state_ref)

    stage = functools.partial(_mix_stage, rows=rows, steps_per_batch=steps_per_batch)

    @pl.when(lax.rem(i, 2) == 0)
    def _():
        stage(buf_a, buf_b, tile, *ins, o_ref, tail_ref, state_ref)

    @pl.when(lax.rem(i, 2) == 1)
    def _():
        stage(buf_b, buf_a, tile, *ins, o_ref, tail_ref, state_ref)


def _mix_layer(x2, mod_l, norm1_g, w_in_p, prm, bsz, seq):
    rows = MIX_ROWS
    steps = seq // rows
    n_tiles = bsz * steps
    t = bsz * seq
    x_spec = pl.BlockSpec((rows, D_MODEL), lambda i: (jnp.minimum(i, n_tiles - 1), 0))
    o_spec = pl.BlockSpec((rows, D_MODEL), lambda i: (jnp.maximum(i - 1, 0), 0))

    def full_spec(shape):
        nd = len(shape)
        return pl.BlockSpec(shape, lambda i: (0,) * nd)

    def mod_spec(k):
        return pl.BlockSpec((None, None, 1, D_MODEL), lambda i: (jnp.minimum(i, n_tiles - 1) // steps, k, 0, 0))

    consts = [prm["ln_g"], prm["ln_b"], prm["ws"], prm["bs_e"], prm["gm_g"], prm["att_g"],
              prm["conv_w"], prm["conv_b"], prm["dt_bias"], prm["a_log"], prm["dskip_e"], prm["ssm_g"], prm["e64"]]

    def p_buffers():
        return [pltpu.VMEM((rows, 2 * GM_WIDTH), F32), pltpu.VMEM((rows, ATT_WIDTH), F32),
                pltpu.VMEM((rows, 2 * KV_WIDTH), BF16), pltpu.VMEM((rows, SSM_WIDTH), F32),
                pltpu.VMEM((rows, LANES), F32), pltpu.VMEM((rows + SUBLANES, CONV_CH), F32)]

    return pl.pallas_call(
        functools.partial(_mix_kernel, rows=rows, steps_per_batch=steps),
        grid=(n_tiles + 1,),
        in_specs=[pl.BlockSpec(memory_space=pltpu.SMEM),
                  x_spec, full_spec((1, D_MODEL)), mod_spec(1), mod_spec(0),
                  pl.BlockSpec((D_MODEL, IN_PAD), lambda i: (0, 0), pipeline_mode=pl.Buffered(1))]
                 + [full_spec(a.shape) for a in consts],
        out_specs=o_spec,
        out_shape=jax.ShapeDtypeStruct((t, D_MODEL), BF16),
        scratch_shapes=p_buffers() + p_buffers()
                       + [pltpu.VMEM((SUBLANES, CONV_CH), F32),
                          pltpu.VMEM((SSM_STATE, SSM_WIDTH), F32)],
        compiler_params=_cparams(("arbitrary",)),
        name="mix_layer",
    )(prm["sinks"], x2, norm1_g.reshape(1, D_MODEL), mod_l, mod_l, w_in_p, *consts)


def _out_kernel(x_ref, m_ref, g1_ref, w_ref, ng_ref, sc_ref, sh_ref, o_ref, h_ref, *, sub):
    for r in range(x_ref.shape[0] // sub):
        rs = slice(r * sub, (r + 1) * sub)
        y = x_ref[rs, :] + g1_ref[...] * jnp.dot(m_ref[rs, :], w_ref[...], preferred_element_type=F32)
        o_ref[rs, :] = y
        h_ref[rs, :] = ((_rms(y) * ng_ref[...]) * (1.0 + sc_ref[...]) + sh_ref[...]).astype(BF16)


def _out_proj(x2, mix, mod_l, w_out_p, norm2_g, seq):
    t = x2.shape[0]
    tm = OUT_TM
    tiles_per_batch = seq // tm
    row = pl.BlockSpec((tm, D_MODEL), lambda i: (i, 0))
    vec = pl.BlockSpec((1, D_MODEL), lambda i: (0, 0))

    def mod_spec(k):
        return pl.BlockSpec((None, None, 1, D_MODEL), lambda i: (i // tiles_per_batch, k, 0, 0))

    return pl.pallas_call(
        functools.partial(_out_kernel, sub=OUT_SUB),
        grid=(t // tm,),
        in_specs=[
            row, row, mod_spec(2),
            pl.BlockSpec((D_MODEL, D_MODEL), lambda i: (0, 0), pipeline_mode=pl.Buffered(1)),
            vec, mod_spec(4), mod_spec(3),
        ],
        out_specs=[row, row],
        out_shape=[jax.ShapeDtypeStruct((t, D_MODEL), F32), jax.ShapeDtypeStruct((t, D_MODEL), BF16)],
        compiler_params=_cparams(("arbitrary",)),
        name="out_proj",
    )(x2, mix, mod_l, w_out_p, norm2_g.reshape(1, D_MODEL), mod_l, mod_l)


def _mlp_kernel(x_ref, h_ref, g2_ref, w1_ref, w2_ref, fg_ref, o_ref, *, final_norm):
    j = pl.program_id(1)

    @pl.when(j == 0)
    def _():
        o_ref[...] = jnp.zeros_like(o_ref)

    a = jnp.dot(h_ref[...], w1_ref[...], preferred_element_type=F32)
    a = jnp.square(jnp.maximum(a, 0.0)).astype(BF16)
    o_ref[...] += jnp.dot(a, w2_ref[...], preferred_element_type=F32)

    @pl.when(j == pl.num_programs(1) - 1)
    def _():
        y = x_ref[...] + g2_ref[...] * o_ref[...]
        if final_norm:
            y = _rms(y) * fg_ref[...]
        o_ref[...] = y


def _mlp(x2, h2, mod_l, w1, w2, final_g, seq, final_norm):
    t = x2.shape[0]
    tm, tf = MLP_TM, MLP_TF
    tiles_per_batch = seq // tm
    row = pl.BlockSpec((tm, D_MODEL), lambda i, j: (i, 0))
    return pl.pallas_call(
        functools.partial(_mlp_kernel, final_norm=final_norm),
        grid=(t // tm, D_FF // tf),
        in_specs=[
            row, row,
            pl.BlockSpec((None, None, 1, D_MODEL), lambda i, j: (i // tiles_per_batch, 5, 0, 0)),
            pl.BlockSpec((D_MODEL, tf), lambda i, j: (0, j)),
            pl.BlockSpec((tf, D_MODEL), lambda i, j: (j, 0)),
            pl.BlockSpec((1, D_MODEL), lambda i, j: (0, 0)),
        ],
        out_specs=row,
        out_shape=jax.ShapeDtypeStruct((t, D_MODEL), F32),
        compiler_params=_cparams(("arbitrary", "arbitrary"), MLP_VMEM_LIMIT),
        name="mlp",
    )(x2, h2, mod_l, w1, w2, final_g.reshape(1, D_MODEL))


def _q_perm():
    idx = []
    for j in range(ATT_HEADS // 2):
        idx += list(range(j * ATT_HEAD_DIM, (j + 1) * ATT_HEAD_DIM))
        idx += list(range((4 + j) * ATT_HEAD_DIM, (5 + j) * ATT_HEAD_DIM))
    return np.asarray(idx, np.int32)


def _expand_matrix():
    e = np.zeros((LANES, SSM_WIDTH), np.float32)
    for piece in range(3):
        for h in range(SSM_HEADS):
            e[piece * SSM_HEADS + h, h * SSM_HEAD_DIM:(h + 1) * SSM_HEAD_DIM] = 1.0
    return jnp.asarray(e, BF16)


def _pad_lanes(v):
    return jnp.pad(v, (0, LANES - v.shape[0])).reshape(1, LANES)


def kernel(x, c, ada_w, ada_b, norm1_g, w_in, gm_ln_g, gm_ln_b, gm_ws, gm_bs, gm_norm_g, attn_sinks, attn_norm_g,
           conv_w, conv_b, dt_bias, a_log, d_skip, ssm_norm_g, w_out, norm2_g, w_mlp1, w_mlp2, final_norm_g):
    bsz, seq, d = x.shape
    n_layers = ada_w.shape[0]
    t = bsz * seq
    qperm = _q_perm()
    e64 = _expand_matrix()

    mod = _modulation(c, ada_w, ada_b).reshape(n_layers, bsz, 6, 1, d)
    x2 = x.reshape(t, d)
    for l in range(n_layers):
        mod_l = mod[l]
        w = w_in[l]
        w_in_p = jnp.concatenate(
            [w[:, :OFF_Q], w[:, OFF_Q:OFF_KV][:, qperm], w[:, OFF_KV:OFF_DT],
             jnp.pad(w[:, OFF_DT:], ((0, 0), (0, LANES - SSM_HEADS)))], axis=1).astype(BF16)
        wo = w_out[l]
        w_out_p = jnp.concatenate(
            [wo[:GM_WIDTH], wo[GM_WIDTH:GM_WIDTH + ATT_WIDTH][qperm], wo[GM_WIDTH + ATT_WIDTH:]], axis=0).astype(BF16)
        prm = {
            "sinks": attn_sinks[l],
            "ln_g": gm_ln_g[l].reshape(1, GM_WIDTH),
            "ln_b": gm_ln_b[l].reshape(1, GM_WIDTH),
            "ws": gm_ws[l],
            "bs_e": jnp.repeat(gm_bs[l].T, LANES, axis=1),
            "gm_g": gm_norm_g[l].reshape(1, GM_WIDTH),
            "att_g": attn_norm_g[l][qperm].reshape(1, ATT_WIDTH),
            "conv_w": conv_w[l],
            "conv_b": conv_b[l].reshape(1, CONV_CH),
            "dt_bias": _pad_lanes(dt_bias[l]),
            "a_log": _pad_lanes(a_log[l]),
            "dskip_e": jnp.repeat(d_skip[l], SSM_HEAD_DIM).reshape(1, SSM_WIDTH),
            "ssm_g": ssm_norm_g[l].reshape(1, SSM_WIDTH),
            "e64": e64,
        }
        mix = _mix_layer(x2, mod_l, norm1_g[l], w_in_p, prm, bsz, seq)
        x2, h2 = _out_proj(x2, mix, mod_l, w_out_p, norm2_g[l], seq)
        x2 = _mlp(x2, h2, mod_l, w_mlp1[l].astype(BF16), w_mlp2[l].astype(BF16), final_norm_g, seq,
                  final_norm=(l == n_layers - 1))
    return x2.reshape(bsz, seq, d)
```

```python
import functools

import numpy as np
import jax
import jax.numpy as jnp
from jax import lax
from jax.experimental import pallas as pl
from jax.experimental.pallas import tpu as pltpu

F32 = jnp.float32
BF16 = jnp.bfloat16

D_MODEL = 2048
CHUNK = 128
GM_WIDTH = 512
GM_HEADS = 4
ATT_WIDTH = 512
ATT_HEADS = 8
ATT_HEAD_DIM = 64
KV_WIDTH = 128
SSM_WIDTH = 1024
SSM_HEADS = 16
SSM_HEAD_DIM = 64
SSM_GROUPS = 2
SSM_STATE = 128
CONV_WIDTH = 4
CONV_CH = 1536
D_FF = 4 * D_MODEL
NEG_INF = -1e30
EPS = 1e-6
LN_EPS = 1e-5
LANES = 128
SUBLANES = 8

OFF_UV, OFF_Q, OFF_KV, OFF_Z, OFF_XBC, OFF_DT = 0, 1024, 1536, 1792, 2816, 4352
IN_PAD = OFF_DT + LANES

VMEM_LIMIT = 56 * 1024 * 1024
MLP_VMEM_LIMIT = 60 * 1024 * 1024

MIX_ROWS = 256
PROJ_PIECE = 256
EMIT = (3, 2, 2, 1, 1)
MIXER_ORDER = "abc"
OUT_TM, OUT_SUB = 512, 256
MLP_TM, MLP_TF = 512, 1024

def _cparams(sem, vmem_limit=VMEM_LIMIT):
    return pltpu.CompilerParams(dimension_semantics=sem, vmem_limit_bytes=vmem_limit)


def _silu(x):
    return x * jax.nn.sigmoid(x)


def _gelu_tanh(x):
    hx = 0.5 * x
    return hx + hx * jnp.tanh(x * (0.7978845608028654 + (0.7978845608028654 * 0.044715) * (x * x)))


def _rms(x):
    return x * lax.rsqrt(jnp.mean(x * x, axis=-1, keepdims=True) + EPS)


def _mod_kernel(c_ref, w_ref, b_ref, o_ref):
    ca = _silu(c_ref[...]).astype(BF16)
    o_ref[...] = jnp.dot(ca, w_ref[...].astype(BF16), preferred_element_type=F32) + b_ref[...]


def _modulation(c, ada_w, ada_b):
    n_layers, d, n = ada_w.shape
    bsz = c.shape[0]
    tn = 1024
    return pl.pallas_call(
        _mod_kernel,
        grid=(n_layers, n // tn),
        in_specs=[
            pl.BlockSpec((bsz, d), lambda l, j: (0, 0)),
            pl.BlockSpec((None, d, tn), lambda l, j: (l, 0, j)),
            pl.BlockSpec((None, 1, tn), lambda l, j: (l, 0, j)),
        ],
        out_specs=pl.BlockSpec((None, bsz, tn), lambda l, j: (l, 0, j)),
        out_shape=jax.ShapeDtypeStruct((n_layers, bsz, n), F32),
        compiler_params=_cparams(("arbitrary", "arbitrary")),
        name="modulation",
    )(c, ada_w, ada_b.reshape(n_layers, 1, n))


def _split3_pack(x, lane):
    x = jnp.where(lane < SSM_HEADS, x, 0.0)
    hi = x.astype(BF16).astype(F32)
    r1 = x - hi
    mid = r1.astype(BF16).astype(F32)
    lo = r1 - mid
    packed = hi + pltpu.roll(mid, SSM_HEADS, axis=1) + pltpu.roll(lo, 2 * SSM_HEADS, axis=1)
    return packed.astype(BF16)


def _proj_pieces(pw, rows):
    puv, pq, pkv, pz, pdt, ext = pw
    body = slice(SUBLANES, SUBLANES + rows)
    every = slice(0, rows)
    pieces = []
    for k in range(CONV_CH // PROJ_PIECE):
        pieces.append((ext, body, slice(k * PROJ_PIECE, (k + 1) * PROJ_PIECE), OFF_XBC + k * PROJ_PIECE, PROJ_PIECE))
    pieces.append((pdt, every, slice(0, LANES), OFF_DT, LANES))
    for k in range(2 * GM_WIDTH // PROJ_PIECE):
        pieces.append((puv, every, slice(k * PROJ_PIECE, (k + 1) * PROJ_PIECE), OFF_UV + k * PROJ_PIECE, PROJ_PIECE))
    pieces.append((pq, every, slice(0, ATT_WIDTH), OFF_Q, ATT_WIDTH))
    pieces.append((pkv, every, slice(0, 2 * KV_WIDTH), OFF_KV, 2 * KV_WIDTH))
    for k in range(SSM_WIDTH // PROJ_PIECE):
        pieces.append((pz, every, slice(k * PROJ_PIECE, (k + 1) * PROJ_PIECE), OFF_Z + k * PROJ_PIECE, PROJ_PIECE))
    return pieces


def _mix_stage(pw, pr, tile, sink_ref, x_ref, ng_ref, sc_ref, sh_ref, w_ref,
               lng_ref, lnb_ref, ws_ref, bs_ref, gmg_ref, ang_ref,
               cw_ref, cb_ref, dtb_ref, alog_ref, dsk_ref, sng_ref, e64_ref,
               o_ref, tail_ref, state_ref, *, rows, steps_per_batch):
    nch = rows // CHUNK
    puv_r, pq_r, pkv_r, pz_r, pdt_r, ext_r = pr
    blk0 = lax.rem(tile + steps_per_batch, steps_per_batch) * nch
    kv_before = jnp.where(blk0 > 0, pw[2][rows - CHUNK:rows, :].astype(F32), 0.0).astype(BF16)

    hb = ((_rms(x_ref[...]) * ng_ref[...]) * (1.0 + sc_ref[...]) + sh_ref[...]).astype(BF16)
    pieces = _proj_pieces(pw, rows)

    def emit_dots(n):
        for _ in range(n):
            if pieces:
                dst, rws, cols, off, width = pieces.pop(0)
                dst[rws, cols] = jnp.dot(hb, w_ref[:, off:off + width], preferred_element_type=F32).astype(dst.dtype)

    lane = lax.broadcasted_iota(jnp.int32, (CHUNK, LANES), 1)
    row = lax.broadcasted_iota(jnp.int32, (CHUNK, LANES), 0)
    causal = row >= lane
    lo_half = lane < ATT_HEAD_DIM
    tril_ones = jnp.where(causal, 1.0, 0.0).astype(BF16)

    emit_dots(EMIT[0])
    ext_r[0:SUBLANES, :] = tail_ref[...]
    ext = ext_r[...]
    acc = cb_ref[...] + cw_ref[CONV_WIDTH - 1:CONV_WIDTH, :] * ext[SUBLANES:SUBLANES + rows, :]
    for d in range(1, CONV_WIDTH):
        k = CONV_WIDTH - 1 - d
        acc = acc + cw_ref[k:k + 1, :] * pltpu.roll(ext, d, axis=0)[SUBLANES:SUBLANES + rows, :]
    xconv = _silu(acc)
    tail_ref[...] = ext[rows:rows + SUBLANES, :]

    a_neg = jnp.where(lane[0:1, :] < SSM_HEADS, -jnp.exp(alog_ref[...]), 0.0)
    e64 = e64_ref[...]

    qi = lax.broadcasted_iota(jnp.int32, (CHUNK, 2 * CHUNK), 0)
    kj = lax.broadcasted_iota(jnp.int32, (CHUNK, 2 * CHUNK), 1)
    diff = qi + CHUNK - kj
    band = (diff >= 0) & (diff < CHUNK)

    def mixer_a(c):
        r0 = c * CHUNK
        rs = slice(r0, r0 + CHUNK)
        emit_dots(EMIT[1])
        u = _gelu_tanh(puv_r[rs, 0:GM_WIDTH])
        v = _gelu_tanh(puv_r[rs, GM_WIDTH:2 * GM_WIDTH])
        ya = []
        for h in range(GM_HEADS):
            hs = slice(h * LANES, (h + 1) * LANES)
            vh = v[:, hs]
            mu = jnp.mean(vh, axis=-1, keepdims=True)
            xc = vh - mu
            var = jnp.mean(xc * xc, axis=-1, keepdims=True)
            vn = xc * lax.rsqrt(var + LN_EPS) * lng_ref[:, hs] + lnb_ref[:, hs]
            w = jnp.where(causal, ws_ref[h], 0.0).astype(BF16)
            gate = jnp.dot(w, vn.astype(BF16), preferred_element_type=F32) + bs_ref[:, hs]
            ya.append(u[:, hs] * gate)
        ya = _rms(jnp.concatenate(ya, axis=1)) * gmg_ref[...]
        o_ref[rs, 0:GM_WIDTH] = ya.astype(o_ref.dtype)

    def mixer_b(c):
        r0 = c * CHUNK
        rs = slice(r0, r0 + CHUNK)
        emit_dots(EMIT[2])
        if c == 0:
            kvp = kv_before
        else:
            kvp = pkv_r[r0 - CHUNK:r0, :]
        kvc = pkv_r[rs, :]
        kcat = jnp.concatenate([kvp[:, 0:KV_WIDTH], kvc[:, 0:KV_WIDTH]], axis=0)
        vcat = jnp.concatenate([kvp[:, KV_WIDTH:], kvc[:, KV_WIDTH:]], axis=0)
        valid = band & ((kj + (blk0 + c - 1) * CHUNK) >= 0)
        ob = []
        for j in range(ATT_HEADS // 2):
            qj = pq_r[rs, j * LANES:(j + 1) * LANES] * (ATT_HEAD_DIM ** -0.5)
            q2 = jnp.concatenate([jnp.where(lo_half, qj, 0.0), jnp.where(lo_half, 0.0, qj)], axis=0).astype(BF16)
            s2 = lax.dot_general(q2, kcat, (((1,), (1,)), ((), ())), preferred_element_type=F32)
            halves = []
            for half in range(2):
                sink = sink_ref[j + 4 * half]
                sc = jnp.where(valid, s2[half * CHUNK:(half + 1) * CHUNK], NEG_INF)
                m = jnp.maximum(jnp.max(sc, axis=-1, keepdims=True), sink)
                e = jnp.exp(sc - m)
                den = jnp.sum(e, axis=-1, keepdims=True) + jnp.exp(sink - m)
                halves.append(jnp.dot(e.astype(BF16), vcat, preferred_element_type=F32) / den)
            ob.append(jnp.where(lo_half, halves[0], halves[1]))
        ob = _rms(jnp.concatenate(ob, axis=1)) * ang_ref[...]
        o_ref[rs, GM_WIDTH:GM_WIDTH + ATT_WIDTH] = ob.astype(o_ref.dtype)

    def mixer_c(c):
        r0 = c * CHUNK
        rs = slice(r0, r0 + CHUNK)
        emit_dots(EMIT[3])
        xs = xconv[rs, 0:SSM_WIDTH]
        dt_in = pdt_r[rs, :] + dtb_ref[...]
        dt = jnp.maximum(dt_in, 0.0) + jnp.log(1.0 + jnp.exp(-jnp.abs(dt_in)))
        da = dt * a_neg
        da_hi = da.astype(BF16)
        r1 = da - da_hi.astype(F32)
        da_mid = r1.astype(BF16)
        da_lo = (r1 - da_mid.astype(F32)).astype(BF16)
        a_cs = (jnp.dot(tril_ones, da_hi, preferred_element_type=F32)
                + jnp.dot(tril_ones, da_mid, preferred_element_type=F32)
                + jnp.dot(tril_ones, da_lo, preferred_element_type=F32))
        a_cs_t = a_cs.T
        ea = jnp.exp(a_cs)
        ds = jnp.exp(a_cs[CHUNK - 1:CHUNK, :] - a_cs)
        dt_e = jnp.dot(_split3_pack(dt, lane), e64, preferred_element_type=F32)
        ds_e = jnp.dot(_split3_pack(ds, lane), e64, preferred_element_type=F32)
        ea_e = jnp.dot(_split3_pack(ea, lane), e64, preferred_element_type=F32)
        xdt = xs * dt_e
        wst = (xdt * ds_e).astype(BF16)
        yc = []
        for g in range(SSM_GROUPS):
            emit_dots(EMIT[4])
            gs = slice(g * 512, (g + 1) * 512)
            bg = xconv[rs, SSM_WIDTH + g * SSM_STATE:SSM_WIDTH + (g + 1) * SSM_STATE]
            cg = xconv[rs, SSM_WIDTH + 256 + g * SSM_STATE:SSM_WIDTH + 256 + (g + 1) * SSM_STATE]
            bgb = bg.astype(BF16)
            cgb = cg.astype(BF16)
            cb = lax.dot_general(cgb, bgb, (((1,), (1,)), ((), ())), preferred_element_type=F32)
            bgt = bg.T.astype(BF16)
            st_new = jnp.dot(bgt, wst[:, gs], preferred_element_type=F32)
            prev = state_ref[:, gs]
            y_off = jnp.dot(cgb, prev.astype(BF16), preferred_element_type=F32) * ea_e[:, gs]
            state_ref[:, gs] = prev * ea_e[CHUNK - 1:CHUNK, gs] + st_new
            for jp in range(4):
                h0 = g * 8 + 2 * jp
                ms = []
                for h in (h0, h0 + 1):
                    seg = a_cs[:, h:h + 1] - a_cs_t[h:h + 1, :]
                    ms.append(cb * jnp.exp(jnp.where(causal, seg, -jnp.inf)))
                m2 = jnp.concatenate(ms, axis=1).astype(BF16)
                xp = xdt[:, (h0 // 2) * LANES:(h0 // 2 + 1) * LANES]
                x2 = jnp.concatenate([jnp.where(lo_half, xp, 0.0), jnp.where(lo_half, 0.0, xp)], axis=0).astype(BF16)
                yd = jnp.dot(m2, x2, preferred_element_type=F32)
                yc.append(yd + y_off[:, jp * LANES:(jp + 1) * LANES])
        y = jnp.concatenate(yc, axis=1) + xs * dsk_ref[...]
        y = y * _silu(pz_r[rs, :])
        yn = jnp.concatenate([_rms(y[:, g * 512:(g + 1) * 512]) for g in range(SSM_GROUPS)], axis=1) * sng_ref[...]
        o_ref[rs, GM_WIDTH + ATT_WIDTH:] = yn.astype(o_ref.dtype)

    for mixer in MIXER_ORDER:
        for c in range(nch):
            {"a": mixer_a, "b": mixer_b, "c": mixer_c}[mixer](c)
    emit_dots(len(pieces))


def _mix_kernel(*refs, rows, steps_per_batch):
    n_in = 19
    ins, o_ref, scratch = refs[:n_in], refs[n_in], refs[n_in + 1:]
    buf_a, buf_b = scratch[0:6], scratch[6:12]
    tail_ref, state_ref = scratch[12:14]
    i = pl.program_id(0)
    tile = i - 1

    @pl.when(i == 0)
    def _():
        for r in buf_a + buf_b:
            r[...] = jnp.zeros_like(r)

    @pl.when((i == 0) | (lax.rem(tile + steps_per_batch, steps_per_batch) == 0))
    def _():
        tail_ref[...] = jnp.zeros_like(tail_ref)
        state_ref[...] = jnp.zeros_like(state_ref)

    stage = functools.partial(_mix_stage, rows=rows, steps_per_batch=steps_per_batch)

    @pl.when(lax.rem(i, 2) == 0)
    def _():
        stage(buf_a, buf_b, tile, *ins, o_ref, tail_ref, state_ref)

    @pl.when(lax.rem(i, 2) == 1)
    def _():
        stage(buf_b, buf_a, tile, *ins, o_ref, tail_ref, state_ref)


def _mix_layer(x2, mod_l, norm1_g, w_in_p, prm, bsz, seq):
    rows = MIX_ROWS
    steps = seq // rows
    n_tiles = bsz * steps
    t = bsz * seq
    x_spec = pl.BlockSpec((rows, D_MODEL), lambda i: (jnp.minimum(i, n_tiles - 1), 0))
    o_spec = pl.BlockSpec((rows, D_MODEL), lambda i: (jnp.maximum(i - 1, 0), 0))

    def full_spec(shape):
        nd = len(shape)
        return pl.BlockSpec(shape, lambda i: (0,) * nd)

    def mod_spec(k):
        return pl.BlockSpec((None, None, 1, D_MODEL), lambda i: (jnp.minimum(i, n_tiles - 1) // steps, k, 0, 0))

    consts = [prm["ln_g"], prm["ln_b"], prm["ws"], prm["bs_e"], prm["gm_g"], prm["att_g"],
              prm["conv_w"], prm["conv_b"], prm["dt_bias"], prm["a_log"], prm["dskip_e"], prm["ssm_g"], prm["e64"]]

    def p_buffers():
        return [pltpu.VMEM((rows, 2 * GM_WIDTH), F32), pltpu.VMEM((rows, ATT_WIDTH), F32),
                pltpu.VMEM((rows, 2 * KV_WIDTH), BF16), pltpu.VMEM((rows, SSM_WIDTH), F32),
                pltpu.VMEM((rows, LANES), F32), pltpu.VMEM((rows + SUBLANES, CONV_CH), F32)]

    return pl.pallas_call(
        functools.partial(_mix_kernel, rows=rows, steps_per_batch=steps),
        grid=(n_tiles + 1,),
        in_specs=[pl.BlockSpec(memory_space=pltpu.SMEM),
                  x_spec, full_spec((1, D_MODEL)), mod_spec(1), mod_spec(0),
                  pl.BlockSpec((D_MODEL, IN_PAD), lambda i: (0, 0), pipeline_mode=pl.Buffered(1))]
                 + [full_spec(a.shape) for a in consts],
        out_specs=o_spec,
        out_shape=jax.ShapeDtypeStruct((t, D_MODEL), BF16),
        scratch_shapes=p_buffers() + p_buffers()
                       + [pltpu.VMEM((SUBLANES, CONV_CH), F32),
                          pltpu.VMEM((SSM_STATE, SSM_WIDTH), F32)],
        compiler_params=_cparams(("arbitrary",)),
        name="mix_layer",
    )(prm["sinks"], x2, norm1_g.reshape(1, D_MODEL), mod_l, mod_l, w_in_p, *consts)


def _out_kernel(x_ref, m_ref, g1_ref, w_ref, ng_ref, sc_ref, sh_ref, o_ref, h_ref, *, sub):
    for r in range(x_ref.shape[0] // sub):
        rs = slice(r * sub, (r + 1) * sub)
        y = x_ref[rs, :] + g1_ref[...] * jnp.dot(m_ref[rs, :], w_ref[...], preferred_element_type=F32)
        o_ref[rs, :] = y
        h_ref[rs, :] = ((_rms(y) * ng_ref[...]) * (1.0 + sc_ref[...]) + sh_ref[...]).astype(BF16)


def _out_proj(x2, mix, mod_l, w_out_p, norm2_g, seq):
    t = x2.shape[0]
    tm = OUT_TM
    tiles_per_batch = seq // tm
    row = pl.BlockSpec((tm, D_MODEL), lambda i: (i, 0))
    vec = pl.BlockSpec((1, D_MODEL), lambda i: (0, 0))

    def mod_spec(k):
        return pl.BlockSpec((None, None, 1, D_MODEL), lambda i: (i // tiles_per_batch, k, 0, 0))

    return pl.pallas_call(
        functools.partial(_out_kernel, sub=OUT_SUB),
        grid=(t // tm,),
        in_specs=[
            row, row, mod_spec(2),
            pl.BlockSpec((D_MODEL, D_MODEL), lambda i: (0, 0), pipeline_mode=pl.Buffered(1)),
            vec, mod_spec(4), mod_spec(3),
        ],
        out_specs=[row, row],
        out_shape=[jax.ShapeDtypeStruct((t, D_MODEL), F32), jax.ShapeDtypeStruct((t, D_MODEL), BF16)],
        compiler_params=_cparams(("arbitrary",)),
        name="out_proj",
    )(x2, mix, mod_l, w_out_p, norm2_g.reshape(1, D_MODEL), mod_l, mod_l)


def _mlp_kernel(x_ref, h_ref, g2_ref, w1_ref, w2_ref, fg_ref, o_ref, *, final_norm):
    j = pl.program_id(1)

    @pl.when(j == 0)
    def _():
        o_ref[...] = jnp.zeros_like(o_ref)

    a = jnp.dot(h_ref[...], w1_ref[...], preferred_element_type=F32)
    a = jnp.square(jnp.maximum(a, 0.0)).astype(BF16)
    o_ref[...] += jnp.dot(a, w2_ref[...], preferred_element_type=F32)

    @pl.when(j == pl.num_programs(1) - 1)
    def _():
        y = x_ref[...] + g2_ref[...] * o_ref[...]
        if final_norm:
            y = _rms(y) * fg_ref[...]
        o_ref[...] = y


def _mlp(x2, h2, mod_l, w1, w2, final_g, seq, final_norm):
    t = x2.shape[0]
    tm, tf = MLP_TM, MLP_TF
    tiles_per_batch = seq // tm
    row = pl.BlockSpec((tm, D_MODEL), lambda i, j: (i, 0))
    return pl.pallas_call(
        functools.partial(_mlp_kernel, final_norm=final_norm),
        grid=(t // tm, D_FF // tf),
        in_specs=[
            row, row,
            pl.BlockSpec((None, None, 1, D_MODEL), lambda i, j: (i // tiles_per_batch, 5, 0, 0)),
            pl.BlockSpec((D_MODEL, tf), lambda i, j: (0, j)),
            pl.BlockSpec((tf, D_MODEL), lambda i, j: (j, 0)),
            pl.BlockSpec((1, D_MODEL), lambda i, j: (0, 0)),
        ],
        out_specs=row,
        out_shape=jax.ShapeDtypeStruct((t, D_MODEL), F32),
        compiler_params=_cparams(("arbitrary", "arbitrary"), MLP_VMEM_LIMIT),
        name="mlp",
    )(x2, h2, mod_l, w1, w2, final_g.reshape(1, D_MODEL))


def _q_perm():
    idx = []
    for j in range(ATT_HEADS // 2):
        idx += list(range(j * ATT_HEAD_DIM, (j + 1) * ATT_HEAD_DIM))
        idx += list(range((4 + j) * ATT_HEAD_DIM, (5 + j) * ATT_HEAD_DIM))
    return np.asarray(idx, np.int32)


def _expand_matrix():
    e = np.zeros((LANES, SSM_WIDTH), np.float32)
    for piece in range(3):
        for h in range(SSM_HEADS):
            e[piece * SSM_HEADS + h, h * SSM_HEAD_DIM:(h + 1) * SSM_HEAD_DIM] = 1.0
    return jnp.asarray(e, BF16)


def _pad_lanes(v):
    return jnp.pad(v, (0, LANES - v.shape[0])).reshape(1, LANES)


def kernel(x, c, ada_w, ada_b, norm1_g, w_in, gm_ln_g, gm_ln_b, gm_ws, gm_bs, gm_norm_g, attn_sinks, attn_norm_g,
           conv_w, conv_b, dt_bias, a_log, d_skip, ssm_norm_g, w_out, norm2_g, w_mlp1, w_mlp2, final_norm_g):
    bsz, seq, d = x.shape
    n_layers = ada_w.shape[0]
    t = bsz * seq
    qperm = _q_perm()
    e64 = _expand_matrix()

    mod = _modulation(c, ada_w, ada_b).reshape(n_layers, bsz, 6, 1, d)
    x2 = x.reshape(t, d)
    for l in range(n_layers):
        mod_l = mod[l]
        w = w_in[l]
        w_in_p = jnp.concatenate(
            [w[:, :OFF_Q], w[:, OFF_Q:OFF_KV][:, qperm], w[:, OFF_KV:OFF_DT],
             jnp.pad(w[:, OFF_DT:], ((0, 0), (0, LANES - SSM_HEADS)))], axis=1).astype(BF16)
        wo = w_out[l]
        w_out_p = jnp.concatenate(
            [wo[:GM_WIDTH], wo[GM_WIDTH:GM_WIDTH + ATT_WIDTH][qperm], wo[GM_WIDTH + ATT_WIDTH:]], axis=0).astype(BF16)
        prm = {
            "sinks": attn_sinks[l],
            "ln_g": gm_ln_g[l].reshape(1, GM_WIDTH),
            "ln_b": gm_ln_b[l].reshape(1, GM_WIDTH),
            "ws": gm_ws[l],
            "bs_e": jnp.repeat(gm_bs[l].T, LANES, axis=1),
            "gm_g": gm_norm_g[l].reshape(1, GM_WIDTH),
            "att_g": attn_norm_g[l][qperm].reshape(1, ATT_WIDTH),
            "conv_w": conv_w[l],
            "conv_b": conv_b[l].reshape(1, CONV_CH),
            "dt_bias": _pad_lanes(dt_bias[l]),
            "a_log": _pad_lanes(a_log[l]),
            "dskip_e": jnp.repeat(d_skip[l], SSM_HEAD_DIM).reshape(1, SSM_WIDTH),
            "ssm_g": ssm_norm_g[l].reshape(1, SSM_WIDTH),
            "e64": e64,
        }
        mix = _mix_layer(x2, mod_l, norm1_g[l], w_in_p, prm, bsz, seq)
        x2, h2 = _out_proj(x2, mix, mod_l, w_out_p, norm2_g[l], seq)
        x2 = _mlp(x2, h2, mod_l, w_mlp1[l].astype(BF16), w_mlp2[l].astype(BF16), final_norm_g, seq,
                  final_norm=(l == n_layers - 1))
    return x2.reshape(bsz, seq, d)
```

```python
import functools

import numpy as np
import jax
import jax.numpy as jnp
from jax import lax
from jax.experimental import pallas as pl
from jax.experimental.pallas import tpu as pltpu

F32 = jnp.float32
BF16 = jnp.bfloat16

D_MODEL = 2048
CHUNK = 128
GM_WIDTH = 512
GM_HEADS = 4
ATT_WIDTH = 512
ATT_HEADS = 8
ATT_HEAD_DIM = 64
KV_WIDTH = 128
SSM_WIDTH = 1024
SSM_HEADS = 16
SSM_HEAD_DIM = 64
SSM_GROUPS = 2
SSM_STATE = 128
CONV_WIDTH = 4
CONV_CH = 1536
D_FF = 4 * D_MODEL
NEG_INF = -1e30
EPS = 1e-6
LN_EPS = 1e-5
LANES = 128
SUBLANES = 8

OFF_UV, OFF_Q, OFF_KV, OFF_Z, OFF_XBC, OFF_DT = 0, 1024, 1536, 1792, 2816, 4352
IN_PAD = OFF_DT + LANES

VMEM_LIMIT = 56 * 1024 * 1024
MLP_VMEM_LIMIT = 60 * 1024 * 1024

MIX_ROWS = 256
PROJ_PIECE = 256
EMIT = (3, 2, 2, 1, 1)
MIXER_ORDER = ("a0", "a1", "b0", "b1", "c0", "c1")
MOD_TN = 2048
OUT_TM, OUT_SUB = 512, 256
MLP_TM, MLP_TF = 512, 2048

def _cparams(sem, vmem_limit=VMEM_LIMIT):
    return pltpu.CompilerParams(dimension_semantics=sem, vmem_limit_bytes=vmem_limit)


def _silu(x):
    return x * jax.nn.sigmoid(x)


def _gelu_tanh(x):
    hx = 0.5 * x
    return hx + hx * jnp.tanh(x * (0.7978845608028654 + (0.7978845608028654 * 0.044715) * (x * x)))


def _rms(x):
    return x * lax.rsqrt(jnp.mean(x * x, axis=-1, keepdims=True) + EPS)


def _mod_kernel(c_ref, w_ref, b_ref, o_ref):
    ca = _silu(c_ref[...]).astype(BF16)
    o_ref[...] = jnp.dot(ca, w_ref[...].astype(BF16), preferred_element_type=F32) + b_ref[...]


def _modulation(c, ada_w, ada_b):
    n_layers, d, n = ada_w.shape
    bsz = c.shape[0]
    tn = MOD_TN
    return pl.pallas_call(
        _mod_kernel,
        grid=(n_layers, n // tn),
        in_specs=[
            pl.BlockSpec((bsz, d), lambda l, j: (0, 0)),
            pl.BlockSpec((None, d, tn), lambda l, j: (l, 0, j)),
            pl.BlockSpec((None, 1, tn), lambda l, j: (l, 0, j)),
        ],
        out_specs=pl.BlockSpec((None, bsz, tn), lambda l, j: (l, 0, j)),
        out_shape=jax.ShapeDtypeStruct((n_layers, bsz, n), F32),
        compiler_params=_cparams(("arbitrary", "arbitrary")),
        name="modulation",
    )(c, ada_w, ada_b.reshape(n_layers, 1, n))


def _split3_pack(x, lane):
    x = jnp.where(lane < SSM_HEADS, x, 0.0)
    hi = x.astype(BF16).astype(F32)
    r1 = x - hi
    mid = r1.astype(BF16).astype(F32)
    lo = r1 - mid
    packed = hi + pltpu.roll(mid, SSM_HEADS, axis=1) + pltpu.roll(lo, 2 * SSM_HEADS, axis=1)
    return packed.astype(BF16)


def _proj_pieces(pw, rows):
    puv, pq, pkv, pz, pdt, ext = pw
    body = slice(SUBLANES, SUBLANES + rows)
    every = slice(0, rows)
    pieces = []
    for k in range(CONV_CH // PROJ_PIECE):
        pieces.append((ext, body, slice(k * PROJ_PIECE, (k + 1) * PROJ_PIECE), OFF_XBC + k * PROJ_PIECE, PROJ_PIECE))
    pieces.append((pdt, every, slice(0, LANES), OFF_DT, LANES))
    for k in range(2 * GM_WIDTH // PROJ_PIECE):
        pieces.append((puv, every, slice(k * PROJ_PIECE, (k + 1) * PROJ_PIECE), OFF_UV + k * PROJ_PIECE, PROJ_PIECE))
    pieces.append((pq, every, slice(0, ATT_WIDTH), OFF_Q, ATT_WIDTH))
    pieces.append((pkv, every, slice(0, 2 * KV_WIDTH), OFF_KV, 2 * KV_WIDTH))
    for k in range(SSM_WIDTH // PROJ_PIECE):
        pieces.append((pz, every, slice(k * PROJ_PIECE, (k + 1) * PROJ_PIECE), OFF_Z + k * PROJ_PIECE, PROJ_PIECE))
    return pieces


def _mix_stage(pw, pr, tile, sink_ref, x_ref, ng_ref, sc_ref, sh_ref, w_ref,
               lng_ref, lnb_ref, ws_ref, bs_ref, gmg_ref, ang_ref,
               cw_ref, cb_ref, dtb_ref, alog_ref, dsk_ref, sng_ref, e64_ref,
               o_ref, tail_ref, state_ref, *, rows, steps_per_batch):
    nch = rows // CHUNK
    puv_r, pq_r, pkv_r, pz_r, pdt_r, ext_r = pr
    blk0 = lax.rem(tile + steps_per_batch, steps_per_batch) * nch
    kv_before = jnp.where(blk0 > 0, pw[2][rows - CHUNK:rows, :].astype(F32), 0.0).astype(BF16)

    hb = ((_rms(x_ref[...]) * ng_ref[...]) * (1.0 + sc_ref[...]) + sh_ref[...]).astype(BF16)
    pieces = _proj_pieces(pw, rows)

    def emit_dots(n):
        for _ in range(n):
            if pieces:
                dst, rws, cols, off, width = pieces.pop(0)
                dst[rws, cols] = jnp.dot(hb, w_ref[:, off:off + width], preferred_element_type=F32).astype(dst.dtype)

    lane = lax.broadcasted_iota(jnp.int32, (CHUNK, LANES), 1)
    row = lax.broadcasted_iota(jnp.int32, (CHUNK, LANES), 0)
    causal = row >= lane
    lo_half = lane < ATT_HEAD_DIM
    tril_ones = jnp.where(causal, 1.0, 0.0).astype(BF16)

    emit_dots(EMIT[0])
    ext_r[0:SUBLANES, :] = tail_ref[...]
    ext = ext_r[...]
    acc = cb_ref[...] + cw_ref[CONV_WIDTH - 1:CONV_WIDTH, :] * ext[SUBLANES:SUBLANES + rows, :]
    for d in range(1, CONV_WIDTH):
        k = CONV_WIDTH - 1 - d
        acc = acc + cw_ref[k:k + 1, :] * pltpu.roll(ext, d, axis=0)[SUBLANES:SUBLANES + rows, :]
    xconv = _silu(acc)
    tail_ref[...] = ext[rows:rows + SUBLANES, :]

    a_neg = jnp.where(lane[0:1, :] < SSM_HEADS, -jnp.exp(alog_ref[...]), 0.0)
    e64 = e64_ref[...]

    qi = lax.broadcasted_iota(jnp.int32, (CHUNK, 2 * CHUNK), 0)
    kj = lax.broadcasted_iota(jnp.int32, (CHUNK, 2 * CHUNK), 1)
    diff = qi + CHUNK - kj
    band = (diff >= 0) & (diff < CHUNK)

    def mixer_a(c):
        r0 = c * CHUNK
        rs = slice(r0, r0 + CHUNK)
        emit_dots(EMIT[1])
        u = _gelu_tanh(puv_r[rs, 0:GM_WIDTH])
        v = _gelu_tanh(puv_r[rs, GM_WIDTH:2 * GM_WIDTH])
        ya = []
        for h in range(GM_HEADS):
            hs = slice(h * LANES, (h + 1) * LANES)
            vh = v[:, hs]
            mu = jnp.mean(vh, axis=-1, keepdims=True)
            xc = vh - mu
            var = jnp.mean(xc * xc, axis=-1, keepdims=True)
            vn = xc * lax.rsqrt(var + LN_EPS) * lng_ref[:, hs] + lnb_ref[:, hs]
            w = jnp.where(causal, ws_ref[h], 0.0).astype(BF16)
            gate = jnp.dot(w, vn.astype(BF16), preferred_element_type=F32) + bs_ref[:, hs]
            ya.append(u[:, hs] * gate)
        ya = _rms(jnp.concatenate(ya, axis=1)) * gmg_ref[...]
        o_ref[rs, 0:GM_WIDTH] = ya.astype(o_ref.dtype)

    def mixer_b(c):
        r0 = c * CHUNK
        rs = slice(r0, r0 + CHUNK)
        emit_dots(EMIT[2])
        if c == 0:
            kvp = kv_before
        else:
            kvp = pkv_r[r0 - CHUNK:r0, :]
        kvc = pkv_r[rs, :]
        kcat = jnp.concatenate([kvp[:, 0:KV_WIDTH], kvc[:, 0:KV_WIDTH]], axis=0)
        vcat = jnp.concatenate([kvp[:, KV_WIDTH:], kvc[:, KV_WIDTH:]], axis=0)
        valid = band & ((kj + (blk0 + c - 1) * CHUNK) >= 0)
        ob = []
        for j in range(ATT_HEADS // 2):
            qj = pq_r[rs, j * LANES:(j + 1) * LANES] * (ATT_HEAD_DIM ** -0.5)
            q2 = jnp.concatenate([jnp.where(lo_half, qj, 0.0), jnp.where(lo_half, 0.0, qj)], axis=0).astype(BF16)
            s2 = lax.dot_general(q2, kcat, (((1,), (1,)), ((), ())), preferred_element_type=F32)
            halves = []
            for half in range(2):
                sink = sink_ref[j + 4 * half]
                sc = jnp.where(valid, s2[half * CHUNK:(half + 1) * CHUNK], NEG_INF)
                m = jnp.maximum(jnp.max(sc, axis=-1, keepdims=True), sink)
                e = jnp.exp(sc - m)
                den = jnp.sum(e, axis=-1, keepdims=True) + jnp.exp(sink - m)
                halves.append(jnp.dot(e.astype(BF16), vcat, preferred_element_type=F32) / den)
            ob.append(jnp.where(lo_half, halves[0], halves[1]))
        ob = _rms(jnp.concatenate(ob, axis=1)) * ang_ref[...]
        o_ref[rs, GM_WIDTH:GM_WIDTH + ATT_WIDTH] = ob.astype(o_ref.dtype)

    def mixer_c(c):
        r0 = c * CHUNK
        rs = slice(r0, r0 + CHUNK)
        emit_dots(EMIT[3])
        xs = xconv[rs, 0:SSM_WIDTH]
        dt_in = pdt_r[rs, :] + dtb_ref[...]
        dt = jnp.maximum(dt_in, 0.0) + jnp.log(1.0 + jnp.exp(-jnp.abs(dt_in)))
        da = dt * a_neg
        da_hi = da.astype(BF16)
        r1 = da - da_hi.astype(F32)
        da_mid = r1.astype(BF16)
        da_lo = (r1 - da_mid.astype(F32)).astype(BF16)
        a_cs = (jnp.dot(tril_ones, da_hi, preferred_element_type=F32)
                + jnp.dot(tril_ones, da_mid, preferred_element_type=F32)
                + jnp.dot(tril_ones, da_lo, preferred_element_type=F32))
        a_cs_t = a_cs.T
        ea = jnp.exp(a_cs)
        ds = jnp.exp(a_cs[CHUNK - 1:CHUNK, :] - a_cs)
        dt_e = jnp.dot(_split3_pack(dt, lane), e64, preferred_element_type=F32)
        ds_e = jnp.dot(_split3_pack(ds, lane), e64, preferred_element_type=F32)
        ea_e = jnp.dot(_split3_pack(ea, lane), e64, preferred_element_type=F32)
        xdt = xs * dt_e
        wst = (xdt * ds_e).astype(BF16)
        yc = []
        for g in range(SSM_GROUPS):
            emit_dots(EMIT[4])
            gs = slice(g * 512, (g + 1) * 512)
            bg = xconv[rs, SSM_WIDTH + g * SSM_STATE:SSM_WIDTH + (g + 1) * SSM_STATE]
            cg = xconv[rs, SSM_WIDTH + 256 + g * SSM_STATE:SSM_WIDTH + 256 + (g + 1) * SSM_STATE]
            bgb = bg.astype(BF16)
            cgb = cg.astype(BF16)
            cb = lax.dot_general(cgb, bgb, (((1,), (1,)), ((), ())), preferred_element_type=F32)
            bgt = bg.T.astype(BF16)
            st_new = jnp.dot(bgt, wst[:, gs], preferred_element_type=F32)
            prev = state_ref[:, gs]
            y_off = jnp.dot(cgb, prev.astype(BF16), preferred_element_type=F32) * ea_e[:, gs]
            state_ref[:, gs] = prev * ea_e[CHUNK - 1:CHUNK, gs] + st_new
            for jp in range(4):
                h0 = g * 8 + 2 * jp
                ms = []
                for h in (h0, h0 + 1):
                    seg = a_cs[:, h:h + 1] - a_cs_t[h:h + 1, :]
                    ms.append(cb * jnp.exp(jnp.where(causal, seg, -jnp.inf)))
                m2 = jnp.concatenate(ms, axis=1).astype(BF16)
                xp = xdt[:, (h0 // 2) * LANES:(h0 // 2 + 1) * LANES]
                x2 = jnp.concatenate([jnp.where(lo_half, xp, 0.0), jnp.where(lo_half, 0.0, xp)], axis=0).astype(BF16)
                yd = jnp.dot(m2, x2, preferred_element_type=F32)
                yc.append(yd + y_off[:, jp * LANES:(jp + 1) * LANES])
        y = jnp.concatenate(yc, axis=1) + xs * dsk_ref[...]
        y = y * _silu(pz_r[rs, :])
        yn = jnp.concatenate([_rms(y[:, g * 512:(g + 1) * 512]) for g in range(SSM_GROUPS)], axis=1) * sng_ref[...]
        o_ref[rs, GM_WIDTH + ATT_WIDTH:] = yn.astype(o_ref.dtype)

    assert sorted(MIXER_ORDER) == sorted(m + str(c) for m in "abc" for c in range(nch))
    for item in MIXER_ORDER:
        {"a": mixer_a, "b": mixer_b, "c": mixer_c}[item[0]](int(item[1]))
    emit_dots(len(pieces))


def _mix_kernel(*refs, rows, steps_per_batch):
    n_in = 19
    ins, o_ref, scratch = refs[:n_in], refs[n_in], refs[n_in + 1:]
    buf_a, buf_b = scratch[0:6], scratch[6:12]
    tail_ref, state_ref = scratch[12:14]
    i = pl.program_id(0)
    tile = i - 1

    @pl.when(i == 0)
    def _():
        for r in buf_a + buf_b:
            r[...] = jnp.zeros_like(r)

    @pl.when((i == 0) | (lax.rem(tile + steps_per_batch, steps_per_batch) == 0))
    def _():
        tail_ref[...] = jnp.zeros_like(tail_ref)
        state_ref[...] = jnp.zeros_like(state_ref)

    stage = functools.partial(_mix_stage, rows=rows, steps_per_batch=steps_per_batch)

    @pl.when(lax.rem(i, 2) == 0)
    def _():
        stage(buf_a, buf_b, tile, *ins, o_ref, tail_ref, state_ref)

    @pl.when(lax.rem(i, 2) == 1)
    def _():
        stage(buf_b, buf_a, tile, *ins, o_ref, tail_ref, state_ref)


def _mix_layer(x2, mod_l, norm1_g, w_in_p, prm, bsz, seq):
    rows = MIX_ROWS
    steps = seq // rows
    n_tiles = bsz * steps
    t = bsz * seq
    x_spec = pl.BlockSpec((rows, D_MODEL), lambda i: (jnp.minimum(i, n_tiles - 1), 0))
    o_spec = pl.BlockSpec((rows, D_MODEL), lambda i: (jnp.maximum(i - 1, 0), 0))

    def full_spec(shape):
        nd = len(shape)
        return pl.BlockSpec(shape, lambda i: (0,) * nd)

    def mod_spec(k):
        return pl.BlockSpec((None, None, 1, D_MODEL), lambda i: (jnp.minimum(i, n_tiles - 1) // steps, k, 0, 0))

    consts = [prm["ln_g"], prm["ln_b"], prm["ws"], prm["bs_e"], prm["gm_g"], prm["att_g"],
              prm["conv_w"], prm["conv_b"], prm["dt_bias"], prm["a_log"], prm["dskip_e"], prm["ssm_g"], prm["e64"]]

    def p_buffers():
        return [pltpu.VMEM((rows, 2 * GM_WIDTH), F32), pltpu.VMEM((rows, ATT_WIDTH), F32),
                pltpu.VMEM((rows, 2 * KV_WIDTH), BF16), pltpu.VMEM((rows, SSM_WIDTH), F32),
                pltpu.VMEM((rows, LANES), F32), pltpu.VMEM((rows + SUBLANES, CONV_CH), F32)]

    return pl.pallas_call(
        functools.partial(_mix_kernel, rows=rows, steps_per_batch=steps),
        grid=(n_tiles + 1,),
        in_specs=[pl.BlockSpec(memory_space=pltpu.SMEM),
                  x_spec, full_spec((1, D_MODEL)), mod_spec(1), mod_spec(0),
                  pl.BlockSpec((D_MODEL, IN_PAD), lambda i: (0, 0), pipeline_mode=pl.Buffered(1))]
                 + [full_spec(a.shape) for a in consts],
        out_specs=o_spec,
        out_shape=jax.ShapeDtypeStruct((t, D_MODEL), BF16),
        scratch_shapes=p_buffers() + p_buffers()
                       + [pltpu.VMEM((SUBLANES, CONV_CH), F32),
                          pltpu.VMEM((SSM_STATE, SSM_WIDTH), F32)],
        compiler_params=_cparams(("arbitrary",)),
        name="mix_layer",
    )(prm["sinks"], x2, norm1_g.reshape(1, D_MODEL), mod_l, mod_l, w_in_p, *consts)


def _out_kernel(x_ref, m_ref, g1_ref, w_ref, ng_ref, sc_ref, sh_ref, o_ref, h_ref, *, sub):
    for r in range(x_ref.shape[0] // sub):
        rs = slice(r * sub, (r + 1) * sub)
        y = x_ref[rs, :] + g1_ref[...] * jnp.dot(m_ref[rs, :], w_ref[...], preferred_element_type=F32)
        o_ref[rs, :] = y
        h_ref[rs, :] = ((_rms(y) * ng_ref[...]) * (1.0 + sc_ref[...]) + sh_ref[...]).astype(BF16)


def _out_proj(x2, mix, mod_l, w_out_p, norm2_g, seq):
    t = x2.shape[0]
    tm = OUT_TM
    tiles_per_batch = seq // tm
    row = pl.BlockSpec((tm, D_MODEL), lambda i: (i, 0))
    vec = pl.BlockSpec((1, D_MODEL), lambda i: (0, 0))

    def mod_spec(k):
        return pl.BlockSpec((None, None, 1, D_MODEL), lambda i: (i // tiles_per_batch, k, 0, 0))

    return pl.pallas_call(
        functools.partial(_out_kernel, sub=OUT_SUB),
        grid=(t // tm,),
        in_specs=[
            row, row, mod_spec(2),
            pl.BlockSpec((D_MODEL, D_MODEL), lambda i: (0, 0), pipeline_mode=pl.Buffered(1)),
            vec, mod_spec(4), mod_spec(3),
        ],
        out_specs=[row, row],
        out_shape=[jax.ShapeDtypeStruct((t, D_MODEL), F32), jax.ShapeDtypeStruct((t, D_MODEL), BF16)],
        compiler_params=_cparams(("arbitrary",)),
        name="out_proj",
    )(x2, mix, mod_l, w_out_p, norm2_g.reshape(1, D_MODEL), mod_l, mod_l)


def _mlp_kernel(x_ref, h_ref, g2_ref, w1_ref, w2_ref, fg_ref, o_ref, *, final_norm):
    j = pl.program_id(1)
    last = pl.num_programs(1) - 1

    def contribution():
        a = jnp.dot(h_ref[...], w1_ref[...], preferred_element_type=F32)
        a = jnp.square(jnp.maximum(a, 0.0)).astype(BF16)
        return jnp.dot(a, w2_ref[...], preferred_element_type=F32)

    @pl.when(j == 0)
    def _():
        o_ref[...] = contribution()

    @pl.when((j > 0) & (j < last))
    def _():
        o_ref[...] += contribution()

    @pl.when(j == last)
    def _():
        y = x_ref[...] + g2_ref[...] * (o_ref[...] + contribution())
        if final_norm:
            y = _rms(y) * fg_ref[...]
        o_ref[...] = y


def _mlp(x2, h2, mod_l, w1, w2, final_g, seq, final_norm):
    t = x2.shape[0]
    tm, tf = MLP_TM, MLP_TF
    tiles_per_batch = seq // tm
    row = pl.BlockSpec((tm, D_MODEL), lambda i, j: (i, 0))
    return pl.pallas_call(
        functools.partial(_mlp_kernel, final_norm=final_norm),
        grid=(t // tm, D_FF // tf),
        in_specs=[
            row, row,
            pl.BlockSpec((None, None, 1, D_MODEL), lambda i, j: (i // tiles_per_batch, 5, 0, 0)),
            pl.BlockSpec((D_MODEL, tf), lambda i, j: (0, j)),
            pl.BlockSpec((tf, D_MODEL), lambda i, j: (j, 0)),
            pl.BlockSpec((1, D_MODEL), lambda i, j: (0, 0)),
        ],
        out_specs=row,
        out_shape=jax.ShapeDtypeStruct((t, D_MODEL), F32),
        compiler_params=_cparams(("arbitrary", "arbitrary"), MLP_VMEM_LIMIT),
        name="mlp",
    )(x2, h2, mod_l, w1, w2, final_g.reshape(1, D_MODEL))


def _q_perm():
    idx = []
    for j in range(ATT_HEADS // 2):
        idx += list(range(j * ATT_HEAD_DIM, (j + 1) * ATT_HEAD_DIM))
        idx += list(range((4 + j) * ATT_HEAD_DIM, (5 + j) * ATT_HEAD_DIM))
    return np.asarray(idx, np.int32)


def _expand_matrix():
    e = np.zeros((LANES, SSM_WIDTH), np.float32)
    for piece in range(3):
        for h in range(SSM_HEADS):
            e[piece * SSM_HEADS + h, h * SSM_HEAD_DIM:(h + 1) * SSM_HEAD_DIM] = 1.0
    return jnp.asarray(e, BF16)


def _pad_lanes(v):
    return jnp.pad(v, (0, LANES - v.shape[0])).reshape(1, LANES)


def kernel(x, c, ada_w, ada_b, norm1_g, w_in, gm_ln_g, gm_ln_b, gm_ws, gm_bs, gm_norm_g, attn_sinks, attn_norm_g,
           conv_w, conv_b, dt_bias, a_log, d_skip, ssm_norm_g, w_out, norm2_g, w_mlp1, w_mlp2, final_norm_g):
    bsz, seq, d = x.shape
    n_layers = ada_w.shape[0]
    t = bsz * seq
    qperm = _q_perm()
    e64 = _expand_matrix()

    mod = _modulation(c, ada_w, ada_b).reshape(n_layers, bsz, 6, 1, d)
    x2 = x.reshape(t, d)
    for l in range(n_layers):
        mod_l = mod[l]
        w = w_in[l]
        w_in_p = jnp.concatenate(
            [w[:, :OFF_Q], w[:, OFF_Q:OFF_KV][:, qperm], w[:, OFF_KV:OFF_DT],
             jnp.pad(w[:, OFF_DT:], ((0, 0), (0, LANES - SSM_HEADS)))], axis=1).astype(BF16)
        wo = w_out[l]
        w_out_p = jnp.concatenate(
            [wo[:GM_WIDTH], wo[GM_WIDTH:GM_WIDTH + ATT_WIDTH][qperm], wo[GM_WIDTH + ATT_WIDTH:]], axis=0).astype(BF16)
        prm = {
            "sinks": attn_sinks[l],
            "ln_g": gm_ln_g[l].reshape(1, GM_WIDTH),
            "ln_b": gm_ln_b[l].reshape(1, GM_WIDTH),
            "ws": gm_ws[l],
            "bs_e": jnp.repeat(gm_bs[l].T, LANES, axis=1),
            "gm_g": gm_norm_g[l].reshape(1, GM_WIDTH),
            "att_g": attn_norm_g[l][qperm].reshape(1, ATT_WIDTH),
            "conv_w": conv_w[l],
            "conv_b": conv_b[l].reshape(1, CONV_CH),
            "dt_bias": _pad_lanes(dt_bias[l]),
            "a_log": _pad_lanes(a_log[l]),
            "dskip_e": jnp.repeat(d_skip[l], SSM_HEAD_DIM).reshape(1, SSM_WIDTH),
            "ssm_g": ssm_norm_g[l].reshape(1, SSM_WIDTH),
            "e64": e64,
        }
        mix = _mix_layer(x2, mod_l, norm1_g[l], w_in_p, prm, bsz, seq)
        x2, h2 = _out_proj(x2, mix, mod_l, w_out_p, norm2_g[l], seq)
        x2 = _mlp(x2, h2, mod_l, w_mlp1[l].astype(BF16), w_mlp2[l].astype(BF16), final_norm_g, seq,
                  final_norm=(l == n_layers - 1))
    return x2.reshape(bsz, seq, d)
```

```python
import functools

import numpy as np
import jax
import jax.numpy as jnp
from jax import lax
from jax.experimental import pallas as pl
from jax.experimental.pallas import tpu as pltpu

F32 = jnp.float32
BF16 = jnp.bfloat16

D_MODEL = 2048
CHUNK = 128
GM_WIDTH = 512
GM_HEADS = 4
ATT_WIDTH = 512
ATT_HEADS = 8
ATT_HEAD_DIM = 64
KV_WIDTH = 128
SSM_WIDTH = 1024
SSM_HEADS = 16
SSM_HEAD_DIM = 64
SSM_GROUPS = 2
SSM_STATE = 128
CONV_WIDTH = 4
CONV_CH = 1536
D_FF = 4 * D_MODEL
NEG_INF = -1e30
EPS = 1e-6
LN_EPS = 1e-5
LANES = 128
SUBLANES = 8

OFF_UV, OFF_Q, OFF_KV, OFF_Z, OFF_XBC, OFF_DT = 0, 1024, 1536, 1792, 2816, 4352
IN_PAD = OFF_DT + LANES

VMEM_LIMIT = 56 * 1024 * 1024
MLP_VMEM_LIMIT = 60 * 1024 * 1024

MIX_ROWS = 256
PROJ_PIECE = 256
EMIT = (3, 2, 2, 1, 1)
MIXER_ORDER = ("a0", "a1", "b0", "b1", "c0", "c1")
MOD_TK = 256
OUT_TM, OUT_SUB = 512, 256
MLP_TM, MLP_TF = 512, 2048

def _cparams(sem, vmem_limit=VMEM_LIMIT):
    return pltpu.CompilerParams(dimension_semantics=sem, vmem_limit_bytes=vmem_limit)


def _silu(x):
    return x * jax.nn.sigmoid(x)


def _gelu_tanh(x):
    hx = 0.5 * x
    return hx + hx * jnp.tanh(x * (0.7978845608028654 + (0.7978845608028654 * 0.044715) * (x * x)))


def _rms(x):
    return x * lax.rsqrt(jnp.mean(x * x, axis=-1, keepdims=True) + EPS)


def _mod_kernel(c_ref, w_ref, b_ref, o_ref):
    k = pl.program_id(1)
    ca = _silu(c_ref[...]).astype(BF16)
    part = jnp.dot(ca, w_ref[...].astype(BF16), preferred_element_type=F32)

    @pl.when(k == 0)
    def _():
        o_ref[...] = part + b_ref[...]

    @pl.when(k > 0)
    def _():
        o_ref[...] += part


def _modulation(c, ada_w, ada_b):
    n_layers, d, n = ada_w.shape
    bsz = c.shape[0]
    tk = MOD_TK
    return pl.pallas_call(
        _mod_kernel,
        grid=(n_layers, d // tk),
        in_specs=[
            pl.BlockSpec((bsz, tk), lambda l, k: (0, k)),
            pl.BlockSpec((None, tk, n), lambda l, k: (l, k, 0)),
            pl.BlockSpec((None, 1, n), lambda l, k: (l, 0, 0)),
        ],
        out_specs=pl.BlockSpec((None, bsz, n), lambda l, k: (l, 0, 0)),
        out_shape=jax.ShapeDtypeStruct((n_layers, bsz, n), F32),
        compiler_params=_cparams(("arbitrary", "arbitrary")),
        name="modulation",
    )(c, ada_w, ada_b.reshape(n_layers, 1, n))


def _split3_pack(x, lane):
    x = jnp.where(lane < SSM_HEADS, x, 0.0)
    hi = x.astype(BF16).astype(F32)
    r1 = x - hi
    mid = r1.astype(BF16).astype(F32)
    lo = r1 - mid
    packed = hi + pltpu.roll(mid, SSM_HEADS, axis=1) + pltpu.roll(lo, 2 * SSM_HEADS, axis=1)
    return packed.astype(BF16)


def _proj_pieces(pw, rows):
    puv, pq, pkv, pz, pdt, ext = pw
    body = slice(SUBLANES, SUBLANES + rows)
    every = slice(0, rows)
    pieces = []
    for k in range(CONV_CH // PROJ_PIECE):
        pieces.append((ext, body, slice(k * PROJ_PIECE, (k + 1) * PROJ_PIECE), OFF_XBC + k * PROJ_PIECE, PROJ_PIECE))
    pieces.append((pdt, every, slice(0, LANES), OFF_DT, LANES))
    for k in range(2 * GM_WIDTH // PROJ_PIECE):
        pieces.append((puv, every, slice(k * PROJ_PIECE, (k + 1) * PROJ_PIECE), OFF_UV + k * PROJ_PIECE, PROJ_PIECE))
    pieces.append((pq, every, slice(0, ATT_WIDTH), OFF_Q, ATT_WIDTH))
    pieces.append((pkv, every, slice(0, 2 * KV_WIDTH), OFF_KV, 2 * KV_WIDTH))
    for k in range(SSM_WIDTH // PROJ_PIECE):
        pieces.append((pz, every, slice(k * PROJ_PIECE, (k + 1) * PROJ_PIECE), OFF_Z + k * PROJ_PIECE, PROJ_PIECE))
    return pieces


def _mix_stage(pw, pr, tile, sink_ref, x_ref, ng_ref, sc_ref, sh_ref, w_ref,
               lng_ref, lnb_ref, ws_ref, bs_ref, gmg_ref, ang_ref,
               cw_ref, cb_ref, dtb_ref, alog_ref, dsk_ref, sng_ref, e64_ref,
               o_ref, tail_ref, state_ref, *, rows, steps_per_batch):
    nch = rows // CHUNK
    puv_r, pq_r, pkv_r, pz_r, pdt_r, ext_r = pr
    blk0 = lax.rem(tile + steps_per_batch, steps_per_batch) * nch
    kv_before = jnp.where(blk0 > 0, pw[2][rows - CHUNK:rows, :].astype(F32), 0.0).astype(BF16)

    hb = ((_rms(x_ref[...]) * ng_ref[...]) * (1.0 + sc_ref[...]) + sh_ref[...]).astype(BF16)
    pieces = _proj_pieces(pw, rows)

    def emit_dots(n):
        for _ in range(n):
            if pieces:
                dst, rws, cols, off, width = pieces.pop(0)
                dst[rws, cols] = jnp.dot(hb, w_ref[:, off:off + width], preferred_element_type=F32).astype(dst.dtype)

    lane = lax.broadcasted_iota(jnp.int32, (CHUNK, LANES), 1)
    row = lax.broadcasted_iota(jnp.int32, (CHUNK, LANES), 0)
    causal = row >= lane
    lo_half = lane < ATT_HEAD_DIM
    tril_ones = jnp.where(causal, 1.0, 0.0).astype(BF16)

    emit_dots(EMIT[0])
    ext_r[0:SUBLANES, :] = tail_ref[...]
    ext = ext_r[...]
    acc = cb_ref[...] + cw_ref[CONV_WIDTH - 1:CONV_WIDTH, :] * ext[SUBLANES:SUBLANES + rows, :]
    for d in range(1, CONV_WIDTH):
        k = CONV_WIDTH - 1 - d
        acc = acc + cw_ref[k:k + 1, :] * pltpu.roll(ext, d, axis=0)[SUBLANES:SUBLANES + rows, :]
    xconv = _silu(acc)
    tail_ref[...] = ext[rows:rows + SUBLANES, :]

    a_neg = jnp.where(lane[0:1, :] < SSM_HEADS, -jnp.exp(alog_ref[...]), 0.0)
    e64 = e64_ref[...]

    qi = lax.broadcasted_iota(jnp.int32, (CHUNK, 2 * CHUNK), 0)
    kj = lax.broadcasted_iota(jnp.int32, (CHUNK, 2 * CHUNK), 1)
    diff = qi + CHUNK - kj
    band = (diff >= 0) & (diff < CHUNK)

    def mixer_a(c):
        r0 = c * CHUNK
        rs = slice(r0, r0 + CHUNK)
        emit_dots(EMIT[1])
        u = _gelu_tanh(puv_r[rs, 0:GM_WIDTH])
        v = _gelu_tanh(puv_r[rs, GM_WIDTH:2 * GM_WIDTH])
        ya = []
        for h in range(GM_HEADS):
            hs = slice(h * LANES, (h + 1) * LANES)
            vh = v[:, hs]
            mu = jnp.mean(vh, axis=-1, keepdims=True)
            xc = vh - mu
            var = jnp.mean(xc * xc, axis=-1, keepdims=True)
            vn = xc * lax.rsqrt(var + LN_EPS) * lng_ref[:, hs] + lnb_ref[:, hs]
            w = jnp.where(causal, ws_ref[h], 0.0).astype(BF16)
            gate = jnp.dot(w, vn.astype(BF16), preferred_element_type=F32) + bs_ref[:, hs]
            ya.append(u[:, hs] * gate)
        ya = _rms(jnp.concatenate(ya, axis=1)) * gmg_ref[...]
        o_ref[rs, 0:GM_WIDTH] = ya.astype(o_ref.dtype)

    def mixer_b(c):
        r0 = c * CHUNK
        rs = slice(r0, r0 + CHUNK)
        emit_dots(EMIT[2])
        if c == 0:
            kvp = kv_before
        else:
            kvp = pkv_r[r0 - CHUNK:r0, :]
        kvc = pkv_r[rs, :]
        kcat = jnp.concatenate([kvp[:, 0:KV_WIDTH], kvc[:, 0:KV_WIDTH]], axis=0)
        vcat = jnp.concatenate([kvp[:, KV_WIDTH:], kvc[:, KV_WIDTH:]], axis=0)
        valid = band & ((kj + (blk0 + c - 1) * CHUNK) >= 0)
        ob = []
        for j in range(ATT_HEADS // 2):
            qj = pq_r[rs, j * LANES:(j + 1) * LANES] * (ATT_HEAD_DIM ** -0.5)
            q2 = jnp.concatenate([jnp.where(lo_half, qj, 0.0), jnp.where(lo_half, 0.0, qj)], axis=0).astype(BF16)
            s2 = lax.dot_general(q2, kcat, (((1,), (1,)), ((), ())), preferred_element_type=F32)
            halves = []
            for half in range(2):
                sink = sink_ref[j + 4 * half]
                sc = jnp.where(valid, s2[half * CHUNK:(half + 1) * CHUNK], NEG_INF)
                m = jnp.maximum(jnp.max(sc, axis=-1, keepdims=True), sink)
                e = jnp.exp(sc - m)
                den = jnp.sum(e, axis=-1, keepdims=True) + jnp.exp(sink - m)
                halves.append(jnp.dot(e.astype(BF16), vcat, preferred_element_type=F32) / den)
            ob.append(jnp.where(lo_half, halves[0], halves[1]))
        ob = _rms(jnp.concatenate(ob, axis=1)) * ang_ref[...]
        o_ref[rs, GM_WIDTH:GM_WIDTH + ATT_WIDTH] = ob.astype(o_ref.dtype)

    def mixer_c(c):
        r0 = c * CHUNK
        rs = slice(r0, r0 + CHUNK)
        emit_dots(EMIT[3])
        xs = xconv[rs, 0:SSM_WIDTH]
        dt_in = pdt_r[rs, :] + dtb_ref[...]
        dt = jnp.maximum(dt_in, 0.0) + jnp.log(1.0 + jnp.exp(-jnp.abs(dt_in)))
        da = dt * a_neg
        da_hi = da.astype(BF16)
        r1 = da - da_hi.astype(F32)
        da_mid = r1.astype(BF16)
        da_lo = (r1 - da_mid.astype(F32)).astype(BF16)
        a_cs = (jnp.dot(tril_ones, da_hi, preferred_element_type=F32)
                + jnp.dot(tril_ones, da_mid, preferred_element_type=F32)
                + jnp.dot(tril_ones, da_lo, preferred_element_type=F32))
        a_cs_t = a_cs.T
        ea = jnp.exp(a_cs)
        ds = jnp.exp(a_cs[CHUNK - 1:CHUNK, :] - a_cs)
        dt_e = jnp.dot(_split3_pack(dt, lane), e64, preferred_element_type=F32)
        ds_e = jnp.dot(_split3_pack(ds, lane), e64, preferred_element_type=F32)
        ea_e = jnp.dot(_split3_pack(ea, lane), e64, preferred_element_type=F32)
        xdt = xs * dt_e
        wst = (xdt * ds_e).astype(BF16)
        yc = []
        for g in range(SSM_GROUPS):
            emit_dots(EMIT[4])
            gs = slice(g * 512, (g + 1) * 512)
            bg = xconv[rs, SSM_WIDTH + g * SSM_STATE:SSM_WIDTH + (g + 1) * SSM_STATE]
            cg = xconv[rs, SSM_WIDTH + 256 + g * SSM_STATE:SSM_WIDTH + 256 + (g + 1) * SSM_STATE]
            bgb = bg.astype(BF16)
            cgb = cg.astype(BF16)
            cb = lax.dot_general(cgb, bgb, (((1,), (1,)), ((), ())), preferred_element_type=F32)
            bgt = bg.T.astype(BF16)
            st_new = jnp.dot(bgt, wst[:, gs], preferred_element_type=F32)
            prev = state_ref[:, gs]
            y_off = jnp.dot(cgb, prev.astype(BF16), preferred_element_type=F32) * ea_e[:, gs]
            state_ref[:, gs] = prev * ea_e[CHUNK - 1:CHUNK, gs] + st_new
            for jp in range(4):
                h0 = g * 8 + 2 * jp
                ms = []
                for h in (h0, h0 + 1):
                    seg = a_cs[:, h:h + 1] - a_cs_t[h:h + 1, :]
                    ms.append(cb * jnp.exp(jnp.where(causal, seg, -jnp.inf)))
                m2 = jnp.concatenate(ms, axis=1).astype(BF16)
                xp = xdt[:, (h0 // 2) * LANES:(h0 // 2 + 1) * LANES]
                x2 = jnp.concatenate([jnp.where(lo_half, xp, 0.0), jnp.where(lo_half, 0.0, xp)], axis=0).astype(BF16)
                yd = jnp.dot(m2, x2, preferred_element_type=F32)
                yc.append(yd + y_off[:, jp * LANES:(jp + 1) * LANES])
        y = jnp.concatenate(yc, axis=1) + xs * dsk_ref[...]
        y = y * _silu(pz_r[rs, :])
        yn = jnp.concatenate([_rms(y[:, g * 512:(g + 1) * 512]) for g in range(SSM_GROUPS)], axis=1) * sng_ref[...]
        o_ref[rs, GM_WIDTH + ATT_WIDTH:] = yn.astype(o_ref.dtype)

    assert sorted(MIXER_ORDER) == sorted(m + str(c) for m in "abc" for c in range(nch))
    for item in MIXER_ORDER:
        {"a": mixer_a, "b": mixer_b, "c": mixer_c}[item[0]](int(item[1]))
    emit_dots(len(pieces))


def _mix_kernel(*refs, rows, steps_per_batch):
    n_in = 19
    ins, o_ref, scratch = refs[:n_in], refs[n_in], refs[n_in + 1:]
    buf_a, buf_b = scratch[0:6], scratch[6:12]
    tail_ref, state_ref = scratch[12:14]
    i = pl.program_id(0)
    tile = i - 1

    @pl.when(i == 0)
    def _():
        for r in buf_a + buf_b:
            r[...] = jnp.zeros_like(r)

    @pl.when((i == 0) | (lax.rem(tile + steps_per_batch, steps_per_batch) == 0))
    def _():
        tail_ref[...] = jnp.zeros_like(tail_ref)
        state_ref[...] = jnp.zeros_like(state_ref)

    stage = functools.partial(_mix_stage, rows=rows, steps_per_batch=steps_per_batch)

    @pl.when(lax.rem(i, 2) == 0)
    def _():
        stage(buf_a, buf_b, tile, *ins, o_ref, tail_ref, state_ref)

    @pl.when(lax.rem(i, 2) == 1)
    def _():
        stage(buf_b, buf_a, tile, *ins, o_ref, tail_ref, state_ref)


def _mix_layer(x2, mod_l, norm1_g, w_in_p, prm, bsz, seq):
    rows = MIX_ROWS
    steps = seq // rows
    n_tiles = bsz * steps
    t = bsz * seq
    x_spec = pl.BlockSpec((rows, D_MODEL), lambda i: (jnp.minimum(i, n_tiles - 1), 0))
    o_spec = pl.BlockSpec((rows, D_MODEL), lambda i: (jnp.maximum(i - 1, 0), 0))

    def full_spec(shape):
        nd = len(shape)
        return pl.BlockSpec(shape, lambda i: (0,) * nd)

    def mod_spec(k):
        return pl.BlockSpec((None, None, 1, D_MODEL), lambda i: (jnp.minimum(i, n_tiles - 1) // steps, k, 0, 0))

    consts = [prm["ln_g"], prm["ln_b"], prm["ws"], prm["bs_e"], prm["gm_g"], prm["att_g"],
              prm["conv_w"], prm["conv_b"], prm["dt_bias"], prm["a_log"], prm["dskip_e"], prm["ssm_g"], prm["e64"]]

    def p_buffers():
        return [pltpu.VMEM((rows, 2 * GM_WIDTH), F32), pltpu.VMEM((rows, ATT_WIDTH), F32),
                pltpu.VMEM((rows, 2 * KV_WIDTH), BF16), pltpu.VMEM((rows, SSM_WIDTH), F32),
                pltpu.VMEM((rows, LANES), F32), pltpu.VMEM((rows + SUBLANES, CONV_CH), F32)]

    return pl.pallas_call(
        functools.partial(_mix_kernel, rows=rows, steps_per_batch=steps),
        grid=(n_tiles + 1,),
        in_specs=[pl.BlockSpec(memory_space=pltpu.SMEM),
                  x_spec, full_spec((1, D_MODEL)), mod_spec(1), mod_spec(0),
                  pl.BlockSpec((D_MODEL, IN_PAD), lambda i: (0, 0), pipeline_mode=pl.Buffered(1))]
                 + [full_spec(a.shape) for a in consts],
        out_specs=o_spec,
        out_shape=jax.ShapeDtypeStruct((t, D_MODEL), BF16),
        scratch_shapes=p_buffers() + p_buffers()
                       + [pltpu.VMEM((SUBLANES, CONV_CH), F32),
                          pltpu.VMEM((SSM_STATE, SSM_WIDTH), F32)],
        compiler_params=_cparams(("arbitrary",)),
        name="mix_layer",
    )(prm["sinks"], x2, norm1_g.reshape(1, D_MODEL), mod_l, mod_l, w_in_p, *consts)


def _out_kernel(x_ref, m_ref, g1_ref, w_ref, ng_ref, sc_ref, sh_ref, o_ref, h_ref, *, sub):
    for r in range(x_ref.shape[0] // sub):
        rs = slice(r * sub, (r + 1) * sub)
        y = x_ref[rs, :] + g1_ref[...] * jnp.dot(m_ref[rs, :], w_ref[...], preferred_element_type=F32)
        o_ref[rs, :] = y
        h_ref[rs, :] = ((_rms(y) * ng_ref[...]) * (1.0 + sc_ref[...]) + sh_ref[...]).astype(BF16)


def _out_proj(x2, mix, mod_l, w_out_p, norm2_g, seq):
    t = x2.shape[0]
    tm = OUT_TM
    tiles_per_batch = seq // tm
    row = pl.BlockSpec((tm, D_MODEL), lambda i: (i, 0))
    vec = pl.BlockSpec((1, D_MODEL), lambda i: (0, 0))

    def mod_spec(k):
        return pl.BlockSpec((None, None, 1, D_MODEL), lambda i: (i // tiles_per_batch, k, 0, 0))

    return pl.pallas_call(
        functools.partial(_out_kernel, sub=OUT_SUB),
        grid=(t // tm,),
        in_specs=[
            row, row, mod_spec(2),
            pl.BlockSpec((D_MODEL, D_MODEL), lambda i: (0, 0), pipeline_mode=pl.Buffered(1)),
            vec, mod_spec(4), mod_spec(3),
        ],
        out_specs=[row, row],
        out_shape=[jax.ShapeDtypeStruct((t, D_MODEL), F32), jax.ShapeDtypeStruct((t, D_MODEL), BF16)],
        compiler_params=_cparams(("arbitrary",)),
        name="out_proj",
    )(x2, mix, mod_l, w_out_p, norm2_g.reshape(1, D_MODEL), mod_l, mod_l)


def _mlp_kernel(x_ref, h_ref, g2_ref, w1_ref, w2_ref, fg_ref, o_ref, *, final_norm):
    j = pl.program_id(1)
    last = pl.num_programs(1) - 1

    def contribution():
        a = jnp.dot(h_ref[...], w1_ref[...], preferred_element_type=F32)
        a = jnp.square(jnp.maximum(a, 0.0)).astype(BF16)
        return jnp.dot(a, w2_ref[...], preferred_element_type=F32)

    @pl.when(j == 0)
    def _():
        o_ref[...] = contribution()

    @pl.when((j > 0) & (j < last))
    def _():
        o_ref[...] += contribution()

    @pl.when(j == last)
    def _():
        y = x_ref[...] + g2_ref[...] * (o_ref[...] + contribution())
        if final_norm:
            y = _rms(y) * fg_ref[...]
        o_ref[...] = y


def _mlp(x2, h2, mod_l, w1_all, w2_all, layer, final_g, seq, final_norm):
    t = x2.shape[0]
    tm, tf = MLP_TM, MLP_TF
    tiles_per_batch = seq // tm
    row = pl.BlockSpec((tm, D_MODEL), lambda i, j: (i, 0))
    return pl.pallas_call(
        functools.partial(_mlp_kernel, final_norm=final_norm),
        grid=(t // tm, D_FF // tf),
        in_specs=[
            row, row,
            pl.BlockSpec((None, None, 1, D_MODEL), lambda i, j: (i // tiles_per_batch, 5, 0, 0)),
            pl.BlockSpec((None, D_MODEL, tf), lambda i, j: (layer, 0, j)),
            pl.BlockSpec((None, tf, D_MODEL), lambda i, j: (layer, j, 0)),
            pl.BlockSpec((1, D_MODEL), lambda i, j: (0, 0)),
        ],
        out_specs=row,
        out_shape=jax.ShapeDtypeStruct((t, D_MODEL), F32),
        compiler_params=_cparams(("arbitrary", "arbitrary"), MLP_VMEM_LIMIT),
        name="mlp",
    )(x2, h2, mod_l, w1_all, w2_all, final_g.reshape(1, D_MODEL))


def _q_perm():
    idx = []
    for j in range(ATT_HEADS // 2):
        idx += list(range(j * ATT_HEAD_DIM, (j + 1) * ATT_HEAD_DIM))
        idx += list(range((4 + j) * ATT_HEAD_DIM, (5 + j) * ATT_HEAD_DIM))
    return np.asarray(idx, np.int32)


def _expand_matrix():
    e = np.zeros((LANES, SSM_WIDTH), np.float32)
    for piece in range(3):
        for h in range(SSM_HEADS):
            e[piece * SSM_HEADS + h, h * SSM_HEAD_DIM:(h + 1) * SSM_HEAD_DIM] = 1.0
    return jnp.asarray(e, BF16)


def _pad_lanes(v):
    return jnp.pad(v, (0, LANES - v.shape[0])).reshape(1, LANES)


def kernel(x, c, ada_w, ada_b, norm1_g, w_in, gm_ln_g, gm_ln_b, gm_ws, gm_bs, gm_norm_g, attn_sinks, attn_norm_g,
           conv_w, conv_b, dt_bias, a_log, d_skip, ssm_norm_g, w_out, norm2_g, w_mlp1, w_mlp2, final_norm_g):
    bsz, seq, d = x.shape
    n_layers = ada_w.shape[0]
    t = bsz * seq
    qperm = _q_perm()
    e64 = _expand_matrix()

    mod = _modulation(c, ada_w, ada_b).reshape(n_layers, bsz, 6, 1, d)
    x2 = x.reshape(t, d)
    w1_all = w_mlp1.astype(BF16)
    w2_all = w_mlp2.astype(BF16)
    for l in range(n_layers):
        mod_l = mod[l]
        w = w_in[l]
        w_in_p = jnp.concatenate(
            [w[:, :OFF_Q], w[:, OFF_Q:OFF_KV][:, qperm], w[:, OFF_KV:OFF_DT],
             jnp.pad(w[:, OFF_DT:], ((0, 0), (0, LANES - SSM_HEADS)))], axis=1).astype(BF16)
        wo = w_out[l]
        w_out_p = jnp.concatenate(
            [wo[:GM_WIDTH], wo[GM_WIDTH:GM_WIDTH + ATT_WIDTH][qperm], wo[GM_WIDTH + ATT_WIDTH:]], axis=0).astype(BF16)
        prm = {
            "sinks": attn_sinks[l],
            "ln_g": gm_ln_g[l].reshape(1, GM_WIDTH),
            "ln_b": gm_ln_b[l].reshape(1, GM_WIDTH),
            "ws": gm_ws[l],
            "bs_e": jnp.repeat(gm_bs[l].T, LANES, axis=1),
            "gm_g": gm_norm_g[l].reshape(1, GM_WIDTH),
            "att_g": attn_norm_g[l][qperm].reshape(1, ATT_WIDTH),
            "conv_w": conv_w[l],
            "conv_b": conv_b[l].reshape(1, CONV_CH),
            "dt_bias": _pad_lanes(dt_bias[l]),
            "a_log": _pad_lanes(a_log[l]),
            "dskip_e": jnp.repeat(d_skip[l], SSM_HEAD_DIM).reshape(1, SSM_WIDTH),
            "ssm_g": ssm_norm_g[l].reshape(1, SSM_WIDTH),
            "e64": e64,
        }
        mix = _mix_layer(x2, mod_l, norm1_g[l], w_in_p, prm, bsz, seq)
        x2, h2 = _out_proj(x2, mix, mod_l, w_out_p, norm2_g[l], seq)
        x2 = _mlp(x2, h2, mod_l, w1_all, w2_all, l, final_norm_g, seq,
                  final_norm=(l == n_layers - 1))
    return x2.reshape(bsz, seq, d)
```

```python
import functools

import numpy as np
import jax
import jax.numpy as jnp
from jax import lax
from jax.experimental import pallas as pl
from jax.experimental.pallas import tpu as pltpu

F32 = jnp.float32
BF16 = jnp.bfloat16

D_MODEL = 2048
CHUNK = 128
GM_WIDTH = 512
GM_HEADS = 4
ATT_WIDTH = 512
ATT_HEADS = 8
ATT_HEAD_DIM = 64
KV_WIDTH = 128
SSM_WIDTH = 1024
SSM_HEADS = 16
SSM_HEAD_DIM = 64
SSM_GROUPS = 2
SSM_STATE = 128
CONV_WIDTH = 4
CONV_CH = 1536
D_FF = 4 * D_MODEL
NEG_INF = -1e30
EPS = 1e-6
LN_EPS = 1e-5
LANES = 128
SUBLANES = 8

OFF_UV, OFF_Q, OFF_KV, OFF_Z, OFF_XBC, OFF_DT = 0, 1024, 1536, 1792, 2816, 4352
IN_PAD = OFF_DT + LANES

VMEM_LIMIT = 56 * 1024 * 1024
MLP_VMEM_LIMIT = 60 * 1024 * 1024

MIX_ROWS = 256
PROJ_PIECE = 256
EMIT = (3, 2, 2, 1, 1)
MIXER_ORDER = ("a0", "a1", "b0", "b1", "c0", "c1")
MOD_TK = 256
OUT_TM, OUT_SUB = 512, 256
MLP_TM, MLP_TF = 512, 2048

def _cparams(sem, vmem_limit=VMEM_LIMIT):
    return pltpu.CompilerParams(dimension_semantics=sem, vmem_limit_bytes=vmem_limit)


def _silu(x):
    return x * jax.nn.sigmoid(x)


def _gelu_tanh(x):
    hx = 0.5 * x
    return hx + hx * jnp.tanh(x * (0.7978845608028654 + (0.7978845608028654 * 0.044715) * (x * x)))


def _rms(x):
    return x * lax.rsqrt(jnp.mean(x * x, axis=-1, keepdims=True) + EPS)


def _mod_kernel(c_ref, w_ref, b_ref, o_ref):
    k = pl.program_id(1)
    ca = _silu(c_ref[...]).astype(BF16)
    part = jnp.dot(ca, w_ref[...].astype(BF16), preferred_element_type=F32)

    @pl.when(k == 0)
    def _():
        o_ref[...] = part + b_ref[...]

    @pl.when(k > 0)
    def _():
        o_ref[...] += part


def _modulation(c, ada_w, ada_b):
    n_layers, d, n = ada_w.shape
    bsz = c.shape[0]
    tk = MOD_TK
    return pl.pallas_call(
        _mod_kernel,
        grid=(n_layers, d // tk),
        in_specs=[
            pl.BlockSpec((bsz, tk), lambda l, k: (0, k)),
            pl.BlockSpec((None, tk, n), lambda l, k: (l, k, 0)),
            pl.BlockSpec((None, 1, n), lambda l, k: (l, 0, 0)),
        ],
        out_specs=pl.BlockSpec((None, bsz, n), lambda l, k: (l, 0, 0)),
        out_shape=jax.ShapeDtypeStruct((n_layers, bsz, n), F32),
        compiler_params=_cparams(("arbitrary", "arbitrary")),
        name="modulation",
    )(c, ada_w, ada_b.reshape(n_layers, 1, n))


def _split3_pack(x, lane):
    x = jnp.where(lane < SSM_HEADS, x, 0.0)
    hi = x.astype(BF16).astype(F32)
    r1 = x - hi
    mid = r1.astype(BF16).astype(F32)
    lo = r1 - mid
    packed = hi + pltpu.roll(mid, SSM_HEADS, axis=1) + pltpu.roll(lo, 2 * SSM_HEADS, axis=1)
    return packed.astype(BF16)


def _proj_pieces(pw, rows):
    puv, pq, pkv, pz, pdt, ext = pw
    body = slice(SUBLANES, SUBLANES + rows)
    every = slice(0, rows)
    pieces = []
    for k in range(CONV_CH // PROJ_PIECE):
        pieces.append((ext, body, slice(k * PROJ_PIECE, (k + 1) * PROJ_PIECE), OFF_XBC + k * PROJ_PIECE, PROJ_PIECE))
    pieces.append((pdt, every, slice(0, LANES), OFF_DT, LANES))
    for k in range(2 * GM_WIDTH // PROJ_PIECE):
        pieces.append((puv, every, slice(k * PROJ_PIECE, (k + 1) * PROJ_PIECE), OFF_UV + k * PROJ_PIECE, PROJ_PIECE))
    pieces.append((pq, every, slice(0, ATT_WIDTH), OFF_Q, ATT_WIDTH))
    pieces.append((pkv, every, slice(0, 2 * KV_WIDTH), OFF_KV, 2 * KV_WIDTH))
    for k in range(SSM_WIDTH // PROJ_PIECE):
        pieces.append((pz, every, slice(k * PROJ_PIECE, (k + 1) * PROJ_PIECE), OFF_Z + k * PROJ_PIECE, PROJ_PIECE))
    return pieces


def _mix_stage(pw, pr, tile, sink_ref, x_ref, ng_ref, sc_ref, sh_ref, w_ref,
               lng_ref, lnb_ref, ws_ref, bs_ref, gmg_ref, ang_ref,
               cw_ref, cb_ref, dtb_ref, alog_ref, dsk_ref, sng_ref, e64_ref,
               o_ref, tail_ref, state_ref, *, rows, steps_per_batch):
    nch = rows // CHUNK
    puv_r, pq_r, pkv_r, pz_r, pdt_r, ext_r = pr
    blk0 = lax.rem(tile + steps_per_batch, steps_per_batch) * nch
    kv_before = jnp.where(blk0 > 0, pw[2][rows - CHUNK:rows, :].astype(F32), 0.0).astype(BF16)

    hb = ((_rms(x_ref[...]) * ng_ref[...]) * (1.0 + sc_ref[...]) + sh_ref[...]).astype(BF16)
    pieces = _proj_pieces(pw, rows)

    def emit_dots(n):
        for _ in range(n):
            if pieces:
                dst, rws, cols, off, width = pieces.pop(0)
                dst[rws, cols] = jnp.dot(hb, w_ref[:, off:off + width], preferred_element_type=F32).astype(dst.dtype)

    lane = lax.broadcasted_iota(jnp.int32, (CHUNK, LANES), 1)
    row = lax.broadcasted_iota(jnp.int32, (CHUNK, LANES), 0)
    causal = row >= lane
    lo_half = lane < ATT_HEAD_DIM
    tril_ones = jnp.where(causal, 1.0, 0.0).astype(BF16)

    emit_dots(EMIT[0])
    ext_r[0:SUBLANES, :] = tail_ref[...]
    ext = ext_r[...]
    acc = cb_ref[...] + cw_ref[CONV_WIDTH - 1:CONV_WIDTH, :] * ext[SUBLANES:SUBLANES + rows, :]
    for d in range(1, CONV_WIDTH):
        k = CONV_WIDTH - 1 - d
        acc = acc + cw_ref[k:k + 1, :] * pltpu.roll(ext, d, axis=0)[SUBLANES:SUBLANES + rows, :]
    xconv = _silu(acc)
    tail_ref[...] = ext[rows:rows + SUBLANES, :]

    a_neg = jnp.where(lane[0:1, :] < SSM_HEADS, -jnp.exp(alog_ref[...]), 0.0)
    e64 = e64_ref[...]

    qi = lax.broadcasted_iota(jnp.int32, (CHUNK, 2 * CHUNK), 0)
    kj = lax.broadcasted_iota(jnp.int32, (CHUNK, 2 * CHUNK), 1)
    diff = qi + CHUNK - kj
    band = (diff >= 0) & (diff < CHUNK)

    def mixer_a(c):
        r0 = c * CHUNK
        rs = slice(r0, r0 + CHUNK)
        emit_dots(EMIT[1])
        u = _gelu_tanh(puv_r[rs, 0:GM_WIDTH])
        v = _gelu_tanh(puv_r[rs, GM_WIDTH:2 * GM_WIDTH])
        ya = []
        for h in range(GM_HEADS):
            hs = slice(h * LANES, (h + 1) * LANES)
            vh = v[:, hs]
            mu = jnp.mean(vh, axis=-1, keepdims=True)
            xc = vh - mu
            var = jnp.mean(xc * xc, axis=-1, keepdims=True)
            vn = xc * lax.rsqrt(var + LN_EPS) * lng_ref[:, hs] + lnb_ref[:, hs]
            w = jnp.where(causal, ws_ref[h], 0.0).astype(BF16)
            gate = jnp.dot(w, vn.astype(BF16), preferred_element_type=F32) + bs_ref[:, hs]
            ya.append(u[:, hs] * gate)
        ya = _rms(jnp.concatenate(ya, axis=1)) * gmg_ref[...]
        o_ref[rs, 0:GM_WIDTH] = ya.astype(o_ref.dtype)

    def mixer_b(c):
        r0 = c * CHUNK
        rs = slice(r0, r0 + CHUNK)
        emit_dots(EMIT[2])
        if c == 0:
            kvp = kv_before
        else:
            kvp = pkv_r[r0 - CHUNK:r0, :]
        kvc = pkv_r[rs, :]
        kcat = jnp.concatenate([kvp[:, 0:KV_WIDTH], kvc[:, 0:KV_WIDTH]], axis=0)
        vcat = jnp.concatenate([kvp[:, KV_WIDTH:], kvc[:, KV_WIDTH:]], axis=0)
        valid = band & ((kj + (blk0 + c - 1) * CHUNK) >= 0)
        ob = []
        for j in range(ATT_HEADS // 2):
            qj = pq_r[rs, j * LANES:(j + 1) * LANES] * (ATT_HEAD_DIM ** -0.5)
            q2 = jnp.concatenate([jnp.where(lo_half, qj, 0.0), jnp.where(lo_half, 0.0, qj)], axis=0).astype(BF16)
            s2 = lax.dot_general(q2, kcat, (((1,), (1,)), ((), ())), preferred_element_type=F32)
            halves = []
            for half in range(2):
                sink = sink_ref[j + 4 * half]
                sc = jnp.where(valid, s2[half * CHUNK:(half + 1) * CHUNK], NEG_INF)
                m = jnp.maximum(jnp.max(sc, axis=-1, keepdims=True), sink)
                e = jnp.exp(sc - m)
                den = jnp.sum(e, axis=-1, keepdims=True) + jnp.exp(sink - m)
                halves.append(jnp.dot(e.astype(BF16), vcat, preferred_element_type=F32) / den)
            ob.append(jnp.where(lo_half, halves[0], halves[1]))
        ob = _rms(jnp.concatenate(ob, axis=1)) * ang_ref[...]
        o_ref[rs, GM_WIDTH:GM_WIDTH + ATT_WIDTH] = ob.astype(o_ref.dtype)

    def mixer_c(c):
        r0 = c * CHUNK
        rs = slice(r0, r0 + CHUNK)
        emit_dots(EMIT[3])
        xs = xconv[rs, 0:SSM_WIDTH]
        dt_in = pdt_r[rs, :] + dtb_ref[...]
        dt = jnp.maximum(dt_in, 0.0) + jnp.log(1.0 + jnp.exp(-jnp.abs(dt_in)))
        da = dt * a_neg
        da_hi = da.astype(BF16)
        r1 = da - da_hi.astype(F32)
        da_mid = r1.astype(BF16)
        da_lo = (r1 - da_mid.astype(F32)).astype(BF16)
        a_cs = (jnp.dot(tril_ones, da_hi, preferred_element_type=F32)
                + jnp.dot(tril_ones, da_mid, preferred_element_type=F32)
                + jnp.dot(tril_ones, da_lo, preferred_element_type=F32))
        a_cs_t = a_cs.T
        ea = jnp.exp(a_cs)
        ds = jnp.exp(a_cs[CHUNK - 1:CHUNK, :] - a_cs)
        dt_e = jnp.dot(_split3_pack(dt, lane), e64, preferred_element_type=F32)
        ds_e = jnp.dot(_split3_pack(ds, lane), e64, preferred_element_type=F32)
        ea_e = jnp.dot(_split3_pack(ea, lane), e64, preferred_element_type=F32)
        xdt = xs * dt_e
        wst = (xdt * ds_e).astype(BF16)
        yc = []
        for g in range(SSM_GROUPS):
            emit_dots(EMIT[4])
            gs = slice(g * 512, (g + 1) * 512)
            bg = xconv[rs, SSM_WIDTH + g * SSM_STATE:SSM_WIDTH + (g + 1) * SSM_STATE]
            cg = xconv[rs, SSM_WIDTH + 256 + g * SSM_STATE:SSM_WIDTH + 256 + (g + 1) * SSM_STATE]
            bgb = bg.astype(BF16)
            cgb = cg.astype(BF16)
            cb = lax.dot_general(cgb, bgb, (((1,), (1,)), ((), ())), preferred_element_type=F32)
            bgt = bg.T.astype(BF16)
            st_new = jnp.dot(bgt, wst[:, gs], preferred_element_type=F32)
            prev = state_ref[:, gs]
            y_off = jnp.dot(cgb, prev.astype(BF16), preferred_element_type=F32) * ea_e[:, gs]
            state_ref[:, gs] = prev * ea_e[CHUNK - 1:CHUNK, gs] + st_new
            for jp in range(4):
                h0 = g * 8 + 2 * jp
                ms = []
                for h in (h0, h0 + 1):
                    seg = a_cs[:, h:h + 1] - a_cs_t[h:h + 1, :]
                    ms.append(cb * jnp.exp(jnp.where(causal, seg, -jnp.inf)))
                m2 = jnp.concatenate(ms, axis=1).astype(BF16)
                xp = xdt[:, (h0 // 2) * LANES:(h0 // 2 + 1) * LANES]
                x2 = jnp.concatenate([jnp.where(lo_half, xp, 0.0), jnp.where(lo_half, 0.0, xp)], axis=0).astype(BF16)
                yd = jnp.dot(m2, x2, preferred_element_type=F32)
                yc.append(yd + y_off[:, jp * LANES:(jp + 1) * LANES])
        y = jnp.concatenate(yc, axis=1) + xs * dsk_ref[...]
        y = y * _silu(pz_r[rs, :])
        yn = jnp.concatenate([_rms(y[:, g * 512:(g + 1) * 512]) for g in range(SSM_GROUPS)], axis=1) * sng_ref[...]
        o_ref[rs, GM_WIDTH + ATT_WIDTH:] = yn.astype(o_ref.dtype)

    assert sorted(MIXER_ORDER) == sorted(m + str(c) for m in "abc" for c in range(nch))
    for item in MIXER_ORDER:
        {"a": mixer_a, "b": mixer_b, "c": mixer_c}[item[0]](int(item[1]))
    emit_dots(len(pieces))


def _mix_kernel(*refs, rows, steps_per_batch):
    n_in = 19
    ins, o_ref, scratch = refs[:n_in], refs[n_in], refs[n_in + 1:]
    buf_a, buf_b = scratch[0:6], scratch[6:12]
    tail_ref, state_ref = scratch[12:14]
    i = pl.program_id(0)
    tile = i - 1

    @pl.when(i == 0)
    def _():
        for r in buf_a + buf_b:
            r[...] = jnp.zeros_like(r)

    @pl.when((i == 0) | (lax.rem(tile + steps_per_batch, steps_per_batch) == 0))
    def _():
        tail_ref[...] = jnp.zeros_like(tail_ref)
        state_ref[...] = jnp.zeros_like(state_ref)

    stage = functools.partial(_mix_stage, rows=rows, steps_per_batch=steps_per_batch)

    @pl.when(lax.rem(i, 2) == 0)
    def _():
        stage(buf_a, buf_b, tile, *ins, o_ref, tail_ref, state_ref)

    @pl.when(lax.rem(i, 2) == 1)
    def _():
        stage(buf_b, buf_a, tile, *ins, o_ref, tail_ref, state_ref)


def _mix_layer(x2, mod_l, norm1_g, w_in_p, prm, bsz, seq):
    rows = MIX_ROWS
    steps = seq // rows
    n_tiles = bsz * steps
    t = bsz * seq
    x_spec = pl.BlockSpec((rows, D_MODEL), lambda i: (jnp.minimum(i, n_tiles - 1), 0))
    o_spec = pl.BlockSpec((rows, D_MODEL), lambda i: (jnp.maximum(i - 1, 0), 0))

    def full_spec(shape):
        nd = len(shape)
        return pl.BlockSpec(shape, lambda i: (0,) * nd)

    def mod_spec(k):
        return pl.BlockSpec((None, None, 1, D_MODEL), lambda i: (jnp.minimum(i, n_tiles - 1) // steps, k, 0, 0))

    consts = [prm["ln_g"], prm["ln_b"], prm["ws"], prm["bs_e"], prm["gm_g"], prm["att_g"],
              prm["conv_w"], prm["conv_b"], prm["dt_bias"], prm["a_log"], prm["dskip_e"], prm["ssm_g"], prm["e64"]]

    def p_buffers():
        return [pltpu.VMEM((rows, 2 * GM_WIDTH), F32), pltpu.VMEM((rows, ATT_WIDTH), F32),
                pltpu.VMEM((rows, 2 * KV_WIDTH), BF16), pltpu.VMEM((rows, SSM_WIDTH), F32),
                pltpu.VMEM((rows, LANES), F32), pltpu.VMEM((rows + SUBLANES, CONV_CH), F32)]

    return pl.pallas_call(
        functools.partial(_mix_kernel, rows=rows, steps_per_batch=steps),
        grid=(n_tiles + 1,),
        in_specs=[pl.BlockSpec(memory_space=pltpu.SMEM),
                  x_spec, full_spec((1, D_MODEL)), mod_spec(1), mod_spec(0),
                  pl.BlockSpec((D_MODEL, IN_PAD), lambda i: (0, 0), pipeline_mode=pl.Buffered(1))]
                 + [full_spec(a.shape) for a in consts],
        out_specs=o_spec,
        out_shape=jax.ShapeDtypeStruct((t, D_MODEL), BF16),
        scratch_shapes=p_buffers() + p_buffers()
                       + [pltpu.VMEM((SUBLANES, CONV_CH), F32),
                          pltpu.VMEM((SSM_STATE, SSM_WIDTH), F32)],
        compiler_params=_cparams(("arbitrary",)),
        name="mix_layer",
    )(prm["sinks"], x2, norm1_g.reshape(1, D_MODEL), mod_l, mod_l, w_in_p, *consts)


def _out_kernel(x_ref, m_ref, g1_ref, w_ref, ng_ref, sc_ref, sh_ref, o_ref, h_ref, *, sub):
    for r in range(x_ref.shape[0] // sub):
        rs = slice(r * sub, (r + 1) * sub)
        y = x_ref[rs, :] + g1_ref[...] * jnp.dot(m_ref[rs, :], w_ref[...], preferred_element_type=F32)
        o_ref[rs, :] = y
        h_ref[rs, :] = ((_rms(y) * ng_ref[...]) * (1.0 + sc_ref[...]) + sh_ref[...]).astype(BF16)


def _out_proj(x2, mix, mod_l, w_out_p, norm2_g, seq):
    t = x2.shape[0]
    tm = OUT_TM
    tiles_per_batch = seq // tm
    row = pl.BlockSpec((tm, D_MODEL), lambda i: (i, 0))
    vec = pl.BlockSpec((1, D_MODEL), lambda i: (0, 0))

    def mod_spec(k):
        return pl.BlockSpec((None, None, 1, D_MODEL), lambda i: (i // tiles_per_batch, k, 0, 0))

    return pl.pallas_call(
        functools.partial(_out_kernel, sub=OUT_SUB),
        grid=(t // tm,),
        in_specs=[
            row, row, mod_spec(2),
            pl.BlockSpec((D_MODEL, D_MODEL), lambda i: (0, 0), pipeline_mode=pl.Buffered(1)),
            vec, mod_spec(4), mod_spec(3),
        ],
        out_specs=[row, row],
        out_shape=[jax.ShapeDtypeStruct((t, D_MODEL), F32), jax.ShapeDtypeStruct((t, D_MODEL), BF16)],
        compiler_params=_cparams(("arbitrary",)),
        name="out_proj",
    )(x2, mix, mod_l, w_out_p, norm2_g.reshape(1, D_MODEL), mod_l, mod_l)


def _mlp_kernel(x_ref, h_ref, g2_ref, w1_ref, w2_ref, fg_ref, o_ref, *, final_norm):
    j = pl.program_id(1)
    last = pl.num_programs(1) - 1

    def contribution():
        a = jnp.dot(h_ref[...], w1_ref[...], preferred_element_type=F32)
        a = jnp.square(jnp.maximum(a, 0.0)).astype(BF16)
        return jnp.dot(a, w2_ref[...], preferred_element_type=F32)

    @pl.when(j == 0)
    def _():
        o_ref[...] = contribution()

    @pl.when((j > 0) & (j < last))
    def _():
        o_ref[...] += contribution()

    @pl.when(j == last)
    def _():
        y = x_ref[...] + g2_ref[...] * (o_ref[...] + contribution())
        if final_norm:
            y = _rms(y) * fg_ref[...]
        o_ref[...] = y


def _mlp(x2, h2, mod_l, w1_all, w2_all, layer, final_g, seq, final_norm):
    t = x2.shape[0]
    tm, tf = MLP_TM, MLP_TF
    tiles_per_batch = seq // tm
    row = pl.BlockSpec((tm, D_MODEL), lambda i, j: (i, 0))
    return pl.pallas_call(
        functools.partial(_mlp_kernel, final_norm=final_norm),
        grid=(t // tm, D_FF // tf),
        in_specs=[
            row, row,
            pl.BlockSpec((None, None, 1, D_MODEL), lambda i, j: (i // tiles_per_batch, 5, 0, 0)),
            pl.BlockSpec((None, D_MODEL, tf), lambda i, j: (layer, 0, j)),
            pl.BlockSpec((None, tf, D_MODEL), lambda i, j: (layer, j, 0)),
            pl.BlockSpec((1, D_MODEL), lambda i, j: (0, 0)),
        ],
        out_specs=row,
        out_shape=jax.ShapeDtypeStruct((t, D_MODEL), F32),
        compiler_params=_cparams(("arbitrary", "arbitrary"), MLP_VMEM_LIMIT),
        name="mlp",
    )(x2, h2, mod_l, w1_all, w2_all, final_g.reshape(1, D_MODEL))


def _q_perm():
    idx = []
    for j in range(ATT_HEADS // 2):
        idx += list(range(j * ATT_HEAD_DIM, (j + 1) * ATT_HEAD_DIM))
        idx += list(range((4 + j) * ATT_HEAD_DIM, (5 + j) * ATT_HEAD_DIM))
    return np.asarray(idx, np.int32)


def _expand_matrix():
    e = np.zeros((LANES, SSM_WIDTH), np.float32)
    for piece in range(3):
        for h in range(SSM_HEADS):
            e[piece * SSM_HEADS + h, h * SSM_HEAD_DIM:(h + 1) * SSM_HEAD_DIM] = 1.0
    return jnp.asarray(e, BF16)


def _pad_lanes(v):
    return jnp.pad(v, (0, LANES - v.shape[0])).reshape(1, LANES)


def kernel(x, c, ada_w, ada_b, norm1_g, w_in, gm_ln_g, gm_ln_b, gm_ws, gm_bs, gm_norm_g, attn_sinks, attn_norm_g,
           conv_w, conv_b, dt_bias, a_log, d_skip, ssm_norm_g, w_out, norm2_g, w_mlp1, w_mlp2, final_norm_g):
    bsz, seq, d = x.shape
    n_layers = ada_w.shape[0]
    t = bsz * seq
    qperm = _q_perm()
    e64 = _expand_matrix()

    mod = _modulation(c, ada_w, ada_b).reshape(n_layers, bsz, 6, 1, d)
    x2 = x.reshape(t, d)
    w1_all = w_mlp1.astype(BF16)
    w2_all = w_mlp2.astype(BF16)
    w_in_all = w_in.astype(BF16)
    w_out_all = w_out.astype(BF16)
    for l in range(n_layers):
        mod_l = mod[l]
        w = w_in_all[l]
        w_in_p = jnp.concatenate(
            [w[:, :OFF_Q], w[:, OFF_Q:OFF_KV][:, qperm], w[:, OFF_KV:OFF_DT],
             jnp.pad(w[:, OFF_DT:], ((0, 0), (0, LANES - SSM_HEADS)))], axis=1)
        wo = w_out_all[l]
        w_out_p = jnp.concatenate(
            [wo[:GM_WIDTH], wo[GM_WIDTH:GM_WIDTH + ATT_WIDTH][qperm], wo[GM_WIDTH + ATT_WIDTH:]], axis=0)
        prm = {
            "sinks": attn_sinks[l],
            "ln_g": gm_ln_g[l].reshape(1, GM_WIDTH),
            "ln_b": gm_ln_b[l].reshape(1, GM_WIDTH),
            "ws": gm_ws[l],
            "bs_e": jnp.repeat(gm_bs[l].T, LANES, axis=1),
            "gm_g": gm_norm_g[l].reshape(1, GM_WIDTH),
            "att_g": attn_norm_g[l][qperm].reshape(1, ATT_WIDTH),
            "conv_w": conv_w[l],
            "conv_b": conv_b[l].reshape(1, CONV_CH),
            "dt_bias": _pad_lanes(dt_bias[l]),
            "a_log": _pad_lanes(a_log[l]),
            "dskip_e": jnp.repeat(d_skip[l], SSM_HEAD_DIM).reshape(1, SSM_WIDTH),
            "ssm_g": ssm_norm_g[l].reshape(1, SSM_WIDTH),
            "e64": e64,
        }
        mix = _mix_layer(x2, mod_l, norm1_g[l], w_in_p, prm, bsz, seq)
        x2, h2 = _out_proj(x2, mix, mod_l, w_out_p, norm2_g[l], seq)
        x2 = _mlp(x2, h2, mod_l, w1_all, w2_all, l, final_norm_g, seq,
                  final_norm=(l == n_layers - 1))
    return x2.reshape(bsz, seq, d)
```
